```python
import math
import jax, jax.numpy as jnp
from jax import lax
import numpy as np

D_MODEL = 2048
BATCH = 4
SEQ = 2048
DEPTH = 1
DEC_BATCH = 128
DEC_SEQ = 8
PAST_LEN = 8192
PAGE_SIZE = 128

A_HEADS = 8
A_DK = 128
A_DV = 128
A_WIDTH = A_HEADS * A_DV
A_CHUNK = 64
B_HEADS = 8
Q_RANK = 512
KV_RANK = 512
NOPE_DIM = 128
ROPE_DIM = 64
V_DIM = 128
B_WIDTH = B_HEADS * V_DIM
ROPE_THETA = 10000.0
MLA_SCALE = (NOPE_DIM + ROPE_DIM) ** -0.5
Q_BLOCK = 128
D_FF = 4 * D_MODEL
EPS = 1e-6
IN_SIZES = (A_HEADS * A_DK, A_HEADS * A_DK, A_HEADS * A_DV, A_WIDTH, Q_RANK, KV_RANK, ROPE_DIM, D_MODEL, D_MODEL)
IN_COLS = 3 * A_HEADS * A_DK + A_WIDTH + Q_RANK + KV_RANK + ROPE_DIM + 2 * D_MODEL

kernel_name = "hgrn2_mla_gated_hybrid_step"


def rms_norm(x, g):
    xf = x.astype(jnp.float32)
    r = lax.rsqrt(jnp.mean(xf * xf, axis=-1, keepdims=True) + EPS)
    return (xf * r).astype(x.dtype) * g


def rope(x, pos):
    half = ROPE_DIM // 2
    inv = ROPE_THETA ** (-jnp.arange(half, dtype=jnp.float32) / half)
    ang = pos.astype(jnp.float32)[:, None] * inv[None, :]
    shape = (1, ang.shape[0]) + (1,) * (x.ndim - 3) + (half,)
    cos = jnp.cos(ang).reshape(shape).astype(x.dtype)
    sin = jnp.sin(ang).reshape(shape).astype(x.dtype)
    x1, x2 = x[..., :half], x[..., half:]
    return jnp.concatenate([x1 * cos - x2 * sin, x1 * sin + x2 * cos], axis=-1)


def hgrn2_scan(q, k, v, logf, s0):
    B, T, H, _ = q.shape
    C = math.gcd(T, A_CHUNK)
    n = T // C

    def to_chunks(a):
        return a.reshape(B, n, C, H, a.shape[-1]).transpose(1, 0, 3, 2, 4)

    qc, kc, vc, fc = (to_chunks(a) for a in (q, k, v, logf))
    causal = jnp.tril(jnp.ones((C, C), dtype=bool))

    def step(S, inp):
        qi, ki, vi, fi = inp
        b = jnp.cumsum(fi.astype(jnp.float32), axis=2)
        o_inter = jnp.einsum('bhtk,bhkv->bhtv', qi * jnp.exp(b), S)
        diff = b[:, :, :, None, :] - b[:, :, None, :, :]
        decay = jnp.exp(jnp.where(causal[:, :, None], diff, -jnp.inf))
        scores = jnp.einsum('bhtk,bhsk,bhtsk->bhts', qi, ki, decay)
        o = o_inter + jnp.einsum('bhts,bhsv->bhtv', scores, vi)
        b_last = b[:, :, -1:, :]
        S_new = jnp.exp(b_last[:, :, 0, :])[..., None] * S + jnp.einsum(
            'bhsk,bhsv->bhkv', ki * jnp.exp(b_last - b), vi)
        return S_new.astype(jnp.float32), o.astype(jnp.float32)

    S_fin, o = lax.scan(step, s0, (qc, kc, vc, fc))
    o = o.transpose(1, 0, 3, 2, 4).reshape(B, T, H, -1)
    return o, S_fin


def mla_prompt_attention(q_lat, q_pe, ckv, kpe):
    B, T, H, _ = q_lat.shape
    nb = T // Q_BLOCK

    def blocks(a):
        return a.reshape(B, nb, Q_BLOCK, H, a.shape[-1]).swapaxes(0, 1)

    kpos = jnp.arange(T)

    def one(args):
        ql, qp, i = args
        s = (jnp.einsum('bthc,bsc->bhts', ql, ckv) + jnp.einsum('bthr,bsr->bhts', qp, kpe)).astype(jnp.float32) * MLA_SCALE
        qpos = i * Q_BLOCK + jnp.arange(Q_BLOCK)
        s = jnp.where(kpos[None, :] <= qpos[:, None], s, -jnp.inf)
        p = jax.nn.softmax(s, axis=-1).astype(ckv.dtype)
        return jnp.einsum('bhts,bsc->bthc', p, ckv)

    o = lax.map(one, (blocks(q_lat), blocks(q_pe), jnp.arange(nb)))
    return o.swapaxes(0, 1).reshape(B, T, H, -1)


def mla_sample_attention(q_lat, q_pe, ckv_new, kpe_new, cache_ckv_l, cache_kpe_l, page_table):
    Bd, T, H, _ = q_lat.shape
    ckv_past = cache_ckv_l[page_table].reshape(Bd, -1, KV_RANK)
    kpe_past = cache_kpe_l[page_table].reshape(Bd, -1, ROPE_DIM)
    P = ckv_past.shape[1]
    s_past = jnp.einsum('bthc,bsc->bhts', q_lat, ckv_past) + jnp.einsum('bthr,bsr->bhts', q_pe, kpe_past)
    s_new = jnp.einsum('bthc,bsc->bhts', q_lat, ckv_new) + jnp.einsum('bthr,bsr->bhts', q_pe, kpe_new)
    causal = jnp.tril(jnp.ones((T, T), dtype=bool))
    s_new = jnp.where(causal, s_new.astype(jnp.float32), -jnp.inf)
    s = jnp.concatenate([s_past.astype(jnp.float32), s_new], axis=-1) * MLA_SCALE
    p = jax.nn.softmax(s, axis=-1).astype(ckv_new.dtype)
    return (jnp.einsum('bhts,bsc->bthc', p[..., :P], ckv_past)
            + jnp.einsum('bhts,bsc->bthc', p[..., P:], ckv_new))


def setup_inputs(seed: int = 0) -> dict:
    key = jax.random.key(seed)
    ks = jax.random.split(key, 32)
    n_pages = PAST_LEN // PAGE_SIZE
    n_used = DEC_BATCH * n_pages
    n_pool = (n_used * 5) // 4
    f32 = jnp.float32

    def nrm(k, shape, scale):
        return jax.random.normal(k, shape, dtype=f32) * scale

    def gain(k, shape):
        return 1.0 + nrm(k, shape, 0.02)

    perm = jax.random.permutation(ks[0], n_pool)[:n_used]
    page_table = perm.reshape(DEC_BATCH, n_pages).astype(jnp.int32)
    return {
        "x_prompt": nrm(ks[1], (BATCH, SEQ, D_MODEL), 1.0),
        "x_sample": nrm(ks[2], (DEC_BATCH, DEC_SEQ, D_MODEL), 1.0),
        "c_prompt": nrm(ks[3], (BATCH, D_MODEL), 1.0),
        "c_sample": nrm(ks[4], (DEC_BATCH, D_MODEL), 1.0),
        "cache_ckv": nrm(ks[5], (DEPTH, n_pool, PAGE_SIZE, KV_RANK), 1.0),
        "cache_kpe": nrm(ks[6], (DEPTH, n_pool, PAGE_SIZE, ROPE_DIM), 1.0),
        "state_hgrn": nrm(ks[7], (DEPTH, DEC_BATCH, A_HEADS, A_DK, A_DV), 0.3),
        "page_table": page_table,
        "w_ada": nrm(ks[8], (DEPTH, D_MODEL, 6 * D_MODEL), 0.5 * D_MODEL ** -0.5),
        "b_ada": nrm(ks[9], (DEPTH, 6 * D_MODEL), 0.01),
        "g_pre_mix": gain(ks[10], (DEPTH, D_MODEL)),
        "g_post_mix": gain(ks[11], (DEPTH, D_MODEL)),
        "g_pre_mlp": gain(ks[12], (DEPTH, D_MODEL)),
        "g_post_mlp": gain(ks[13], (DEPTH, D_MODEL)),
        "w_in": nrm(ks[14], (DEPTH, D_MODEL, IN_COLS), D_MODEL ** -0.5),
        "lb_logits": nrm(ks[15], (DEPTH + 1, A_HEADS * A_DK), 0.5),
        "g_hgrn_norm": gain(ks[16], (DEPTH, A_WIDTH)),
        "w_a_out": nrm(ks[17], (DEPTH, A_WIDTH, D_MODEL), A_WIDTH ** -0.5),
        "g_q_norm": gain(ks[18], (DEPTH, Q_RANK)),
        "w_q_up": nrm(ks[19], (DEPTH, Q_RANK, B_HEADS * (NOPE_DIM + ROPE_DIM)), Q_RANK ** -0.5),
        "g_kv_norm": gain(ks[20], (DEPTH, KV_RANK)),
        "w_kv_up": nrm(ks[21], (DEPTH, KV_RANK, B_HEADS * (NOPE_DIM + V_DIM)), KV_RANK ** -0.5),
        "w_b_out": nrm(ks[22], (DEPTH, B_WIDTH, D_MODEL), B_WIDTH ** -0.5),
        "w_o": nrm(ks[23], (DEPTH, D_MODEL, D_MODEL), D_MODEL ** -0.5),
        "w_up": nrm(ks[24], (DEPTH, D_MODEL, D_FF), D_MODEL ** -0.5),
        "w_down": nrm(ks[25], (DEPTH, D_FF, D_MODEL), D_FF ** -0.5),
    }


def reference(x_prompt, x_sample, c_prompt, c_sample, cache_ckv, cache_kpe, state_hgrn, page_table,
              w_ada, b_ada, g_pre_mix, g_post_mix, g_pre_mlp, g_post_mlp, w_in, lb_logits, g_hgrn_norm,
              w_a_out, g_q_norm, w_q_up, g_kv_norm, w_kv_up, w_b_out, w_o, w_up, w_down):
    offsets = np.cumsum(IN_SIZES)[:-1].tolist()
    lb_all = jnp.cumsum(jax.nn.softmax(lb_logits.astype(jnp.float32), axis=0), axis=0)

    def run_group(x, c, pos, s0_layers, attend):
        B, T, _ = x.shape
        ckv_out, kpe_out, s_out = [], [], []
        for l in range(DEPTH):
            mod = jax.nn.silu(c) @ w_ada[l] + b_ada[l]
            sh1, sc1, gt1, sh2, sc2, gt2 = (m[:, None, :] for m in jnp.split(mod, 6, axis=-1))
            h = rms_norm(x, g_pre_mix[l]) * (1.0 + sc1) + sh1
            proj = h @ w_in[l]
            qa, fa, ia, ga, qd, kvd, kpe_raw, gate_a, gate_b = jnp.split(proj, offsets, axis=-1)
            lb = lb_all[l].reshape(A_HEADS, A_DK)
            q_h = jax.nn.silu(qa).reshape(B, T, A_HEADS, A_DK)
            f = lb + (1.0 - lb) * jax.nn.sigmoid(fa.astype(jnp.float32).reshape(B, T, A_HEADS, A_DK))
            logf = jnp.log(f)
            k_h = 1.0 - f
            v_h = ia.reshape(B, T, A_HEADS, A_DV)
            o_a, s_fin = hgrn2_scan(q_h, k_h, v_h, logf, s0_layers(l, B))
            o_a = rms_norm(o_a.astype(h.dtype), g_hgrn_norm[l].reshape(A_HEADS, A_DV))
            o_a = o_a * jax.nn.sigmoid(ga.reshape(B, T, A_HEADS, A_DV))
            y_a = o_a.reshape(B, T, A_WIDTH) @ w_a_out[l]
            q_full = (rms_norm(qd, g_q_norm[l]) @ w_q_up[l]).reshape(B, T, B_HEADS, NOPE_DIM + ROPE_DIM)
            q_nope, q_pe = q_full[..., :NOPE_DIM], rope(q_full[..., NOPE_DIM:], pos)
            ckv = rms_norm(kvd, g_kv_norm[l])
            kpe = rope(kpe_raw, pos)
            w_kv = w_kv_up[l].reshape(KV_RANK, B_HEADS, NOPE_DIM + V_DIM)
            w_uk, w_uv = w_kv[..., :NOPE_DIM], w_kv[..., NOPE_DIM:]
            q_lat = jnp.einsum('bthn,chn->bthc', q_nope, w_uk)
            o_lat = attend(q_lat, q_pe, ckv, kpe, l)
            o_b = jnp.einsum('bthc,chv->bthv', o_lat, w_uv).reshape(B, T, B_WIDTH)
            y_b = o_b @ w_b_out[l]
            merged = jax.nn.sigmoid(gate_a) * y_a + jax.nn.sigmoid(gate_b) * y_b
            x = x + gt1 * rms_norm(merged @ w_o[l], g_post_mix[l])
            h2 = rms_norm(x, g_pre_mlp[l]) * (1.0 + sc2) + sh2
            u = jax.nn.relu(h2 @ w_up[l])
            x = x + gt2 * rms_norm((u * u) @ w_down[l], g_post_mlp[l])
            ckv_out.append(ckv)
            kpe_out.append(kpe)
            s_out.append(s_fin)
        return x, jnp.stack(ckv_out), jnp.stack(kpe_out), jnp.stack(s_out)

    def prompt_attend(q_lat, q_pe, ckv, kpe, l):
        return mla_prompt_attention(q_lat, q_pe, ckv, kpe)

    def sample_attend(q_lat, q_pe, ckv, kpe, l):
        return mla_sample_attention(q_lat, q_pe, ckv, kpe, cache_ckv[l], cache_kpe[l], page_table)

    def prompt_s0(l, B):
        return jnp.zeros((B, A_HEADS, A_DK, A_DV), dtype=jnp.float32)

    def sample_s0(l, B):
        return state_hgrn[l].astype(jnp.float32)

    pos_prompt = jnp.arange(x_prompt.shape[1])
    pos_sample = PAST_LEN + jnp.arange(x_sample.shape[1])
    y_prompt, ckv_p, kpe_p, s_p = run_group(x_prompt, c_prompt, pos_prompt, prompt_s0, prompt_attend)
    y_sample, ckv_s, kpe_s, s_s = run_group(x_sample, c_sample, pos_sample, sample_s0, sample_attend)
    return (y_prompt, y_sample, ckv_p, kpe_p, s_p, ckv_s, kpe_s, s_s)
```

```python
import functools
import math

import numpy as np
import jax
import jax.numpy as jnp
from jax import lax
from jax.experimental import pallas as pl
from jax.experimental.pallas import tpu as pltpu

BF = jnp.bfloat16
F32 = jnp.float32

D_MODEL = 2048
A_HEADS = 8
A_DK = 128
A_DV = 128
A_CHUNK = 64
B_HEADS = 8
Q_RANK = 512
KV_RANK = 512
NOPE_DIM = 128
ROPE_DIM = 64
V_DIM = 128
ROPE_THETA = 10000.0
MLA_SCALE = (NOPE_DIM + ROPE_DIM) ** -0.5
D_FF = 4 * D_MODEL
EPS = 1e-6
PAGE_SIZE = 128

LANES = 128
QK_DIM = KV_RANK + LANES
VMEM_LIMIT = 56 * 1024 * 1024

_NT = (((1,), (1,)), ((), ()))
_TN = (((0,), (0,)), ((), ()))


def _dot(a, b):
    return jnp.dot(a, b, preferred_element_type=F32)


def _dot_nt(a, b):
    return lax.dot_general(a, b, _NT, preferred_element_type=F32)


def _dot_tn(a, b):
    return lax.dot_general(a, b, _TN, preferred_element_type=F32)


def _rms(x, g):
    r = lax.rsqrt(jnp.mean(x * x, axis=-1, keepdims=True) + EPS)
    return (x * r) * g


def _params(sem):
    return pltpu.CompilerParams(dimension_semantics=sem, vmem_limit_bytes=VMEM_LIMIT)


def _ada_body(c_ref, w_ref, b_ref, o_ref):
    c = c_ref[...]
    s = (c * jax.nn.sigmoid(c)).astype(BF)
    o_ref[...] = _dot(s, w_ref[...].astype(BF)) + b_ref[...]


def _ada(c_all, w_ada, b_ada, tn=1024):
    m, d = c_all.shape
    n = w_ada.shape[1]
    return pl.pallas_call(
        _ada_body,
        grid=(n // tn,),
        in_specs=[pl.BlockSpec((m, d), lambda j: (0, 0)),
                  pl.BlockSpec((d, tn), lambda j: (0, j)),
                  pl.BlockSpec((1, tn), lambda j: (0, j))],
        out_specs=pl.BlockSpec((m, tn), lambda j: (0, j)),
        out_shape=jax.ShapeDtypeStruct((m, n), F32),
        compiler_params=_params(("arbitrary",)),
        name="ada",
    )(c_all, w_ada, b_ada)


def _proj_body(x_ref, sh_ref, sc_ref, g_ref, w_ref, o_ref, h_ref):
    @pl.when(pl.program_id(2) == 0)
    def _():
        x = x_ref[...]
        h = _rms(x, g_ref[...]) * (1.0 + sc_ref[...]) + sh_ref[...]
        h_ref[...] = h.reshape(h_ref.shape).astype(BF)

    o_ref[...] = _dot(h_ref[...], w_ref[...])


def _proj(x, mod, g, w, gb, tr, tn):
    bm, r, d = x.shape
    n = w.shape[1]
    tm = gb * tr
    nr = r // tr
    return pl.pallas_call(
        _proj_body,
        grid=(bm // gb, nr, n // tn),
        in_specs=[pl.BlockSpec((gb, tr, d), lambda a, i, j: (a, i, 0)),
                  pl.BlockSpec((gb, 1, d), lambda a, i, j: (a, 0, 0)),
                  pl.BlockSpec((gb, 1, d), lambda a, i, j: (a, 0, 1)),
                  pl.BlockSpec((1, d), lambda a, i, j: (0, 0)),
                  pl.BlockSpec((d, tn), lambda a, i, j: (0, j))],
        out_specs=pl.BlockSpec((tm, tn), lambda a, i, j: (a * nr + i, j)),
        out_shape=jax.ShapeDtypeStruct((bm * r, n), F32),
        scratch_shapes=[pltpu.VMEM((tm, d), BF)],
        compiler_params=_params(("arbitrary", "arbitrary", "arbitrary")),
        name="proj",
    )(x, mod, mod, g, w)


_TILE = 64


def _hgrn_consts(chunk):
    t = np.arange(_TILE)[:, None]
    u = np.arange(_TILE)[None, :]
    same_chunk = (t // chunk) == (u // chunk)
    mats = [same_chunk & (u <= t),
            same_chunk & (u > t)]
    masks = []
    m = chunk
    while m >= 2:
        h = m // 2
        ref = (t // m) * m + h
        lower = (t % m) >= h
        mats.append(np.where(lower, (u > ref) & (u <= t), (u > t) & (u <= ref)))
        masks.append(((t // m) == (u // m)) & lower & ((u % m) < h))
        m = h
    masks.append(t == u)
    nm = np.concatenate([a.astype(np.float32) for a in mats], axis=0)
    mk = np.concatenate([a.astype(np.float32) for a in masks], axis=0)
    return jnp.asarray(nm, BF), jnp.asarray(mk, F32)


def _split3(x):
    hi = x.astype(BF)
    r1 = x - hi.astype(F32)
    mid = r1.astype(BF)
    lo = (r1 - mid.astype(F32)).astype(BF)
    return hi, mid, lo


def _hgrn_gates(qa, fa, lb):
    q = qa * jax.nn.sigmoid(qa)
    f = lb + (1.0 - lb) * jax.nn.sigmoid(fa)
    return q, jnp.log(f), 1.0 - f


def _hgrn_lb(lbl):
    e = jnp.exp(lbl - jnp.max(lbl, axis=0, keepdims=True))
    return e[0:1] / jnp.sum(e, axis=0, keepdims=True)


def _hgrn_scores(q, k, e_all, mk_ref, nlev):
    sc = jnp.where(mk_ref[nlev * _TILE:(nlev + 1) * _TILE, :] > 0.5,
                   _dot_nt(q.astype(BF), k.astype(BF)), 0.0)
    for l in range(nlev):
        e = e_all[(2 + l) * _TILE:(3 + l) * _TILE]
        p = _dot_nt((q * e).astype(BF), (k * e).astype(BF))
        sc = sc + jnp.where(mk_ref[l * _TILE:(l + 1) * _TILE, :] > 0.5, p, 0.0)
    return sc


def _hgrn_exps(nm_ref, logf):
    hi, mid, lo = _split3(logf)
    nm = nm_ref[...]
    return jnp.exp(_dot(nm, hi) + _dot(nm, mid) + _dot(nm, lo))


def _hgrn_out(o, ga, gn):
    return (_rms(o, gn) * jax.nn.sigmoid(ga)).astype(BF)


def _hgrn_p_body(qa_ref, fa_ref, ia_ref, ga_ref, lb_ref, gn_ref, nm_ref, mk_ref,
                 o_ref, s_ref, st_ref, *, nchunks, nlev):
    ti = pl.program_id(2)

    @pl.when(ti == 0)
    def _():
        st_ref[...] = jnp.zeros_like(st_ref)

    lb = _hgrn_lb(lb_ref[...])
    gn = gn_ref[...]

    def chunk(c, carry):
        rows = pl.ds(pl.multiple_of(c * _TILE, _TILE), _TILE)
        q, logf, k = _hgrn_gates(qa_ref[0, rows, :], fa_ref[0, rows, :], lb)
        v = ia_ref[0, rows, :].astype(BF)
        e_all = _hgrn_exps(nm_ref, logf)
        eb = e_all[0:_TILE]
        er = e_all[_TILE:2 * _TILE]
        st = st_ref[...]
        o = _dot_nt((q * eb).astype(BF), st.astype(BF))
        sc = _hgrn_scores(q, k, e_all, mk_ref, nlev)
        o = o + _dot(sc.astype(BF), v)
        st_ref[...] = st * eb[_TILE - 1:_TILE, :] + _dot_tn(v, (k * er).astype(BF))
        o_ref[0, rows, :] = _hgrn_out(o, ga_ref[0, rows, :], gn)
        return carry

    lax.fori_loop(0, nchunks, chunk, 0)

    @pl.when(ti == pl.num_programs(2) - 1)
    def _():
        s_ref[0, 0] = st_ref[...].T


def _hgrn_prompt(proj, lb_logits, g_norm, b, t, tt=512):
    nm, mk = _hgrn_consts(A_CHUNK)
    nlev = int(math.log2(A_CHUNK))
    h = A_HEADS
    proj3 = proj.reshape(b, t, proj.shape[-1])

    def col(off):
        return pl.BlockSpec((1, tt, A_DK), lambda bi, hi, ti: (bi, ti, off + hi))

    body = functools.partial(_hgrn_p_body, nchunks=tt // _TILE, nlev=nlev)
    return pl.pallas_call(
        body,
        grid=(b, h, t // tt),
        in_specs=[col(0), col(h), col(2 * h), col(3 * h),
                  pl.BlockSpec((lb_logits.shape[0], A_DK), lambda bi, hi, ti: (0, hi)),
                  pl.BlockSpec((1, A_DV), lambda bi, hi, ti: (0, hi)),
                  pl.BlockSpec(nm.shape, lambda bi, hi, ti: (0, 0)),
                  pl.BlockSpec(mk.shape, lambda bi, hi, ti: (0, 0))],
        out_specs=[pl.BlockSpec((1, tt, A_DV), lambda bi, hi, ti: (bi, ti, hi)),
                   pl.BlockSpec((1, 1, A_DK, A_DV), lambda bi, hi, ti: (bi, hi, 0, 0))],
        out_shape=[jax.ShapeDtypeStruct((b, t, h * A_DV), BF),
                   jax.ShapeDtypeStruct((b, h, A_DK, A_DV), F32)],
        scratch_shapes=[pltpu.VMEM((A_DV, A_DK), F32)],
        compiler_params=_params(("arbitrary", "arbitrary", "arbitrary")),
        name="hgrn_prompt",
    )(proj3, proj3, proj3, proj3, lb_logits, g_norm, nm, mk)


def _hgrn_s_body(qa_ref, fa_ref, ia_ref, ga_ref, lb_ref, gn_ref, nm_ref, mk_ref, sel_ref,
                 s0_ref, o_ref, s_ref, *, nlev, gb):
    lb = _hgrn_lb(lb_ref[...])
    q, logf, k = _hgrn_gates(qa_ref[...], fa_ref[...], lb)
    vf = ia_ref[...]
    v = vf.astype(BF)
    e_all = _hgrn_exps(nm_ref, logf)
    eb = e_all[0:_TILE]
    er = e_all[_TILE:2 * _TILE]
    sel = sel_ref[...]
    hi, mid, lo = _split3(logf)
    selb = sel.astype(BF)
    dec = jnp.exp(_dot_tn(hi, selb) + _dot_tn(mid, selb) + _dot_tn(lo, selb))
    qe = q * eb
    q_blk = (jnp.concatenate([qe] * gb, axis=1) * sel).astype(BF)
    v_blk = (jnp.concatenate([vf] * gb, axis=1) * sel).astype(BF)
    s0 = s0_ref[0, :, 0]
    o = _dot(q_blk, s0.reshape(gb * A_DK, A_DV).astype(BF))
    sc = _hgrn_scores(q, k, e_all, mk_ref, nlev)
    o = o + _dot(sc.astype(BF), v)
    upd = _dot_tn((k * er).astype(BF), v_blk)
    for j in range(gb):
        cols = slice(j * A_DV, (j + 1) * A_DV)
        s_ref[0, j, 0] = dec[:, cols] * s0[j] + upd[:, cols]
    o_ref[...] = _hgrn_out(o, ga_ref[...], gn_ref[...])


def _hgrn_sample(proj, lb_logits, g_norm, state, nb, t):
    gb = _TILE // t
    nm, mk = _hgrn_consts(t)
    nlev = int(math.log2(t))
    h = A_HEADS
    sel = np.zeros((_TILE, gb * A_DV), np.float32)
    for j in range(gb):
        sel[j * t:(j + 1) * t, j * A_DV:(j + 1) * A_DV] = 1.0
    sel = jnp.asarray(sel)

    def col(off):
        return pl.BlockSpec((_TILE, A_DK), lambda i, hi: (i, off + hi))

    body = functools.partial(_hgrn_s_body, nlev=nlev, gb=gb)
    st_spec = pl.BlockSpec((1, gb, 1, A_DK, A_DV), lambda i, hi: (0, i, hi, 0, 0))
    return pl.pallas_call(
        body,
        grid=(nb // gb, h),
        in_specs=[col(0), col(h), col(2 * h), col(3 * h),
                  pl.BlockSpec((lb_logits.shape[0], A_DK), lambda i, hi: (0, hi)),
                  pl.BlockSpec((1, A_DV), lambda i, hi: (0, hi)),
                  pl.BlockSpec(nm.shape, lambda i, hi: (0, 0)),
                  pl.BlockSpec(mk.shape, lambda i, hi: (0, 0)),
                  pl.BlockSpec(sel.shape, lambda i, hi: (0, 0)),
                  st_spec],
        out_specs=[pl.BlockSpec((_TILE, A_DV), lambda i, hi: (i, hi)), st_spec],
        out_shape=[jax.ShapeDtypeStruct((nb * t, h * A_DV), BF),
                   jax.ShapeDtypeStruct(state.shape, F32)],
        compiler_params=_params(("arbitrary", "arbitrary")),
        name="hgrn_sample",
    )(proj, proj, proj, proj, lb_logits, g_norm, nm, mk, sel, state)


def _mla_prep_body(p_ref, cos_ref, sin_ref, gq_ref, gkv_ref, wq_ref, wuk_ref,
                   q_ref, kv_ref, ckv_ref, kpe_ref):
    p = p_ref[...]
    cos = cos_ref[...]
    sin = sin_ref[...]
    qn = _rms(p[:, :Q_RANK], gq_ref[...]).astype(BF)
    qf = _dot(qn, wq_ref[...])
    hw = B_HEADS * NOPE_DIM
    for h in range(B_HEADS):
        c0 = h * NOPE_DIM
        q_lat = _dot(qf[:, c0:c0 + NOPE_DIM].astype(BF), wuk_ref[h])
        q_pe = qf[:, hw + c0:hw + c0 + LANES] * cos + qf[:, 2 * hw + c0:2 * hw + c0 + LANES] * sin
        q_ref[:, h * QK_DIM:h * QK_DIM + KV_RANK] = q_lat.astype(q_ref.dtype)
        q_ref[:, h * QK_DIM + KV_RANK:(h + 1) * QK_DIM] = q_pe.astype(q_ref.dtype)
    ckv = _rms(p[:, Q_RANK:Q_RANK + KV_RANK], gkv_ref[...])
    o = Q_RANK + KV_RANK
    kpe = p[:, o:o + LANES] * cos + p[:, o + LANES:o + 2 * LANES] * sin
    ckv_ref[...] = ckv
    kpe_ref[...] = kpe[:, :ROPE_DIM]
    kv_ref[:, :KV_RANK] = ckv.astype(kv_ref.dtype)
    kv_ref[:, KV_RANK:] = kpe.astype(kv_ref.dtype)


def _mla_prep(proj_mla, cos, sin, g_q, g_kv, wq, wuk, tm, n_pos_tiles, act_dtype):
    rows, pc = proj_mla.shape
    return pl.pallas_call(
        _mla_prep_body,
        grid=(rows // tm,),
        in_specs=[pl.BlockSpec((tm, pc), lambda i: (i, 0)),
                  pl.BlockSpec((tm, LANES), lambda i: (i % n_pos_tiles, 0)),
                  pl.BlockSpec((tm, LANES), lambda i: (i % n_pos_tiles, 0)),
                  pl.BlockSpec((1, Q_RANK), lambda i: (0, 0)),
                  pl.BlockSpec((1, KV_RANK), lambda i: (0, 0)),
                  pl.BlockSpec(wq.shape, lambda i: (0, 0)),
                  pl.BlockSpec(wuk.shape, lambda i: (0, 0, 0))],
        out_specs=[pl.BlockSpec((tm, B_HEADS * QK_DIM), lambda i: (i, 0)),
                   pl.BlockSpec((tm, QK_DIM), lambda i: (i, 0)),
                   pl.BlockSpec((tm, KV_RANK), lambda i: (i, 0)),
                   pl.BlockSpec((tm, ROPE_DIM), lambda i: (i, 0))],
        out_shape=[jax.ShapeDtypeStruct((rows, B_HEADS * QK_DIM), act_dtype),
                   jax.ShapeDtypeStruct((rows, QK_DIM), act_dtype),
                   jax.ShapeDtypeStruct((rows, KV_RANK), F32),
                   jax.ShapeDtypeStruct((rows, ROPE_DIM), F32)],
        compiler_params=_params(("arbitrary",)),
        name="mla_prep",
    )(proj_mla, cos, sin, g_q, g_kv, wq, wuk)


def _softmax_step(s, v, m_ref, l_ref, acc_ref):
    m_prev = m_ref[...]
    m_new = jnp.maximum(m_prev, jnp.max(s, axis=-1, keepdims=True))
    alpha = jnp.exp(m_prev - m_new)
    p = jnp.exp(s - m_new)
    l_ref[...] = alpha * l_ref[...] + jnp.sum(p, axis=-1, keepdims=True)
    acc_ref[...] = alpha * acc_ref[...] + _dot(p.astype(BF), v)
    m_ref[...] = m_new


def _attn_p_body(q_ref, kv_ref, wuv_ref, o_ref, acc_ref, m_ref, l_ref, *, tq):
    qi = pl.program_id(1)
    row = lax.broadcasted_iota(jnp.int32, (tq, tq), 0)
    col = lax.broadcasted_iota(jnp.int32, (tq, tq), 1)
    tri = col <= row
    for h in range(B_HEADS):
        qh = q_ref[0, :, h * QK_DIM:(h + 1) * QK_DIM]
        m_ref[...] = jnp.full_like(m_ref, -jnp.inf)
        l_ref[...] = jnp.zeros_like(l_ref)
        acc_ref[...] = jnp.zeros_like(acc_ref)

        def body(j, carry):
            kv = kv_ref[0, pl.ds(pl.multiple_of(j * tq, tq), tq), :]
            s = _dot_nt(qh, kv) * MLA_SCALE
            s = jnp.where(jnp.logical_or(j < qi, tri), s, -jnp.inf)
            _softmax_step(s, kv[:, :KV_RANK], m_ref, l_ref, acc_ref)
            return carry

        lax.fori_loop(0, qi + 1, body, 0)
        o_lat = (acc_ref[...] / l_ref[...]).astype(BF)
        o_ref[0, :, h * V_DIM:(h + 1) * V_DIM] = _dot(o_lat, wuv_ref[h]).astype(o_ref.dtype)


def _attn_prompt(q_cat, kv_cat, wuv, b, t, tq=256):
    q3 = q_cat.reshape(b, t, q_cat.shape[-1])
    kv3 = kv_cat.reshape(b, t, kv_cat.shape[-1])
    body = functools.partial(_attn_p_body, tq=tq)
    return pl.pallas_call(
        body,
        grid=(b, t // tq),
        in_specs=[pl.BlockSpec((1, tq, q3.shape[-1]), lambda bi, qi: (bi, qi, 0)),
                  pl.BlockSpec((1, t, QK_DIM), lambda bi, qi: (bi, 0, 0)),
                  pl.BlockSpec(wuv.shape, lambda bi, qi: (0, 0, 0))],
        out_specs=pl.BlockSpec((1, tq, B_HEADS * V_DIM), lambda bi, qi: (bi, qi, 0)),
        out_shape=jax.ShapeDtypeStruct((b, t, B_HEADS * V_DIM), BF),
        scratch_shapes=[pltpu.VMEM((tq, KV_RANK), F32),
                        pltpu.VMEM((tq, 1), F32),
                        pltpu.VMEM((tq, 1), F32)],
        compiler_params=_params(("arbitrary", "arbitrary")),
        name="attn_prompt",
    )(q3, kv3, wuv)


def _attn_s_body(pt_ref, q_ref, kvn_ref, ckv_hbm, kpe_hbm, wuv_ref, o_ref,
                 ckv_buf, kpe_buf, sem, qs_ref, acc_ref, m_ref, l_ref,
                 *, nch, ppc, n_pages, t_new):
    b = pl.program_id(0)
    c = pl.program_id(1)
    n = b * nch + c
    total = pl.num_programs(0) * nch
    slot = n % 2

    def page_copies(step, slot_, p):
        page = pt_ref[(step // nch) * n_pages + (step % nch) * ppc + p]
        rows = pl.ds(p * PAGE_SIZE, PAGE_SIZE)
        return (pltpu.make_async_copy(ckv_hbm.at[page], ckv_buf.at[slot_, rows, :], sem.at[slot_, 0]),
                pltpu.make_async_copy(kpe_hbm.at[page], kpe_buf.at[slot_, rows, :], sem.at[slot_, 1]))

    def start_all(step, slot_):
        for p in range(ppc):
            for cp in page_copies(step, slot_, p):
                cp.start()

    @pl.when(n == 0)
    def _():
        start_all(n, slot)

    @pl.when(n + 1 < total)
    def _():
        start_all(n + 1, 1 - slot)

    @pl.when(c == 0)
    def _():
        for h in range(B_HEADS):
            qs_ref[h * t_new:(h + 1) * t_new, :] = q_ref[0, :, h * QK_DIM:(h + 1) * QK_DIM]
        m_ref[...] = jnp.full_like(m_ref, -jnp.inf)
        l_ref[...] = jnp.zeros_like(l_ref)
        acc_ref[...] = jnp.zeros_like(acc_ref)

    for p in range(ppc):
        for cp in page_copies(n, slot, p):
            cp.wait()

    qs = qs_ref[...]
    qb = qs.astype(BF)
    kc = ckv_buf[slot].astype(BF)
    kp = kpe_buf[slot].astype(BF)
    s = (_dot_nt(qb[:, :KV_RANK], kc) + _dot_nt(qb[:, KV_RANK:KV_RANK + ROPE_DIM], kp)) * MLA_SCALE
    _softmax_step(s, kc, m_ref, l_ref, acc_ref)

    @pl.when(c == nch - 1)
    def _():
        kn = kvn_ref[0]
        rows = B_HEADS * t_new
        sn = _dot_nt(qs, kn) * MLA_SCALE
        tq_ = lax.broadcasted_iota(jnp.int32, (rows, t_new), 0) % t_new
        tk_ = lax.broadcasted_iota(jnp.int32, (rows, t_new), 1)
        sn = jnp.where(tk_ <= tq_, sn, -jnp.inf)
        m_prev = m_ref[...]
        m_new = jnp.maximum(m_prev, jnp.max(sn, axis=-1, keepdims=True))
        alpha = jnp.exp(m_prev - m_new)
        pn = jnp.exp(sn - m_new)
        l = alpha * l_ref[...] + jnp.sum(pn, axis=-1, keepdims=True)
        acc = alpha * acc_ref[...]
        vn = kn[:, :KV_RANK].astype(BF).astype(F32)
        pn = pn.astype(BF).astype(F32)
        for j in range(t_new):
            acc = acc + pn[:, j:j + 1] * vn[j:j + 1, :]
        o_lat = (acc / l).astype(BF)
        r = _dot(o_lat, wuv_ref[...])
        for h in range(B_HEADS):
            o_ref[0, :, h * V_DIM:(h + 1) * V_DIM] = r[h * t_new:(h + 1) * t_new, h * V_DIM:(h + 1) * V_DIM]


def _attn_sample(page_table, q_cat, kv_cat, cache_ckv, cache_kpe, wuv_all, nb, t_new, ppc=16):
    n_pages = page_table.shape[1]
    nch = n_pages // ppc
    keys = ppc * PAGE_SIZE
    q3 = q_cat.reshape(nb, t_new, q_cat.shape[-1])
    kv3 = kv_cat.reshape(nb, t_new, kv_cat.shape[-1])
    rows = B_HEADS * t_new
    body = functools.partial(_attn_s_body, nch=nch, ppc=ppc, n_pages=n_pages, t_new=t_new)
    grid_spec = pltpu.PrefetchScalarGridSpec(
        num_scalar_prefetch=1,
        grid=(nb, nch),
        in_specs=[pl.BlockSpec((1, t_new, q3.shape[-1]), lambda bi, ci, pt: (bi, 0, 0)),
                  pl.BlockSpec((1, t_new, QK_DIM), lambda bi, ci, pt: (bi, 0, 0)),
                  pl.BlockSpec(memory_space=pl.ANY),
                  pl.BlockSpec(memory_space=pl.ANY),
                  pl.BlockSpec(wuv_all.shape, lambda bi, ci, pt: (0, 0))],
        out_specs=pl.BlockSpec((1, t_new, B_HEADS * V_DIM), lambda bi, ci, pt: (bi, 0, 0)),
        scratch_shapes=[pltpu.VMEM((2, keys, KV_RANK), F32),
                        pltpu.VMEM((2, keys, ROPE_DIM), F32),
                        pltpu.SemaphoreType.DMA((2, 2)),
                        pltpu.VMEM((rows, QK_DIM), F32),
                        pltpu.VMEM((rows, KV_RANK), F32),
                        pltpu.VMEM((rows, 1), F32),
                        pltpu.VMEM((rows, 1), F32)])
    return pl.pallas_call(
        body,
        grid_spec=grid_spec,
        out_shape=jax.ShapeDtypeStruct((nb, t_new, B_HEADS * V_DIM), F32),
        compiler_params=_params(("arbitrary", "arbitrary")),
        name="attn_sample",
    )(page_table.reshape(-1), q3, kv3, cache_ckv, cache_kpe, wuv_all)


def _mix_body(oa_ref, ob_ref, ga_ref, gb_ref, x_ref, gt_ref, g_ref, wa_ref, wb_ref, wo_ref, o_ref):
    ya = _dot(oa_ref[...].astype(BF), wa_ref[...])
    yb = _dot(ob_ref[...].astype(BF), wb_ref[...])
    merged = jax.nn.sigmoid(ga_ref[...]) * ya + jax.nn.sigmoid(gb_ref[...]) * yb
    z = _dot(merged.astype(BF), wo_ref[...])
    x = x_ref[...]
    o_ref[...] = x + gt_ref[...] * _rms(z, g_ref[...]).reshape(x.shape)


def _mix(o_a, o_b, proj, x, mod, g, wa, wb, wo, gb, tr):
    bm, r, d = x.shape
    tm = gb * tr
    nr = r // tr
    gate0 = 4 * A_HEADS * A_DK // d

    def rows2(width, cb):
        return pl.BlockSpec((tm, width), lambda a, i: (a * nr + i, cb))

    def const(w):
        return pl.BlockSpec(w.shape, lambda a, i: (0, 0))

    return pl.pallas_call(
        _mix_body,
        grid=(bm // gb, nr),
        in_specs=[rows2(o_a.shape[1], 0), rows2(o_b.shape[1], 0),
                  rows2(d, gate0), rows2(d, gate0 + 1),
                  pl.BlockSpec((gb, tr, d), lambda a, i: (a, i, 0)),
                  pl.BlockSpec((gb, 1, d), lambda a, i: (a, 0, 2)),
                  const(g), const(wa), const(wb), const(wo)],
        out_specs=pl.BlockSpec((gb, tr, d), lambda a, i: (a, i, 0)),
        out_shape=jax.ShapeDtypeStruct(x.shape, F32),
        compiler_params=_params(("arbitrary", "arbitrary")),
        name="mix",
    )(o_a, o_b, proj, proj, x, mod, g, wa, wb, wo)


def _mlp_body(x_ref, sh_ref, sc_ref, gt_ref, g1_ref, g2_ref, wu_ref, wd_ref, o_ref, h_ref, acc_ref):
    f = pl.program_id(2)

    @pl.when(f == 0)
    def _():
        x = x_ref[...]
        h = _rms(x, g1_ref[...]) * (1.0 + sc_ref[...]) + sh_ref[...]
        h_ref[...] = h.reshape(h_ref.shape).astype(BF)
        acc_ref[...] = jnp.zeros_like(acc_ref)

    u = jnp.maximum(_dot(h_ref[...], wu_ref[...]), 0.0)
    acc_ref[...] += _dot((u * u).astype(BF), wd_ref[...])

    @pl.when(f == pl.num_programs(2) - 1)
    def _():
        x = x_ref[...]
        o_ref[...] = x + gt_ref[...] * _rms(acc_ref[...], g2_ref[...]).reshape(x.shape)


def _mlp(x, mod, g1, g2, wu, wd, gb, tr, tf=512):
    bm, r, d = x.shape
    tm = gb * tr
    nr = r // tr
    ff = wu.shape[1]

    def modspec(piece):
        return pl.BlockSpec((gb, 1, d), lambda a, i, f: (a, 0, piece))

    return pl.pallas_call(
        _mlp_body,
        grid=(bm // gb, nr, ff // tf),
        in_specs=[pl.BlockSpec((gb, tr, d), lambda a, i, f: (a, i, 0)),
                  modspec(3), modspec(4), modspec(5),
                  pl.BlockSpec((1, d), lambda a, i, f: (0, 0)),
                  pl.BlockSpec((1, d), lambda a, i, f: (0, 0)),
                  pl.BlockSpec((d, tf), lambda a, i, f: (0, f)),
                  pl.BlockSpec((tf, d), lambda a, i, f: (f, 0))],
        out_specs=pl.BlockSpec((gb, tr, d), lambda a, i, f: (a, i, 0)),
        out_shape=jax.ShapeDtypeStruct(x.shape, F32),
        scratch_shapes=[pltpu.VMEM((tm, d), BF), pltpu.VMEM((tm, d), F32)],
        compiler_params=_params(("arbitrary", "arbitrary", "arbitrary")),
        name="mlp",
    )(x, mod, mod, mod, g1, g2, wu, wd)


def _rope_tables(pos):
    half = ROPE_DIM // 2
    inv = ROPE_THETA ** (-jnp.arange(half, dtype=F32) / half)
    ang = pos.astype(F32)[:, None] * inv[None, :]
    z = jnp.zeros((pos.shape[0], LANES - ROPE_DIM), F32)
    cos = jnp.cos(ang)
    sin = jnp.sin(ang)
    return jnp.concatenate([cos, cos, z], axis=1), jnp.concatenate([sin, sin, z], axis=1)


def _swap_halves(w):
    half = w.shape[-1] // 2
    return jnp.concatenate([-w[..., half:], w[..., :half]], axis=-1)


def _pad_lanes(w):
    pad = [(0, 0)] * (w.ndim - 1) + [(0, LANES - w.shape[-1])]
    return jnp.pad(w, pad)


def kernel(x_prompt, x_sample, c_prompt, c_sample, cache_ckv, cache_kpe, state_hgrn, page_table, w_ada, b_ada, g_pre_mix, g_post_mix, g_pre_mlp, g_post_mlp, w_in, lb_logits, g_hgrn_norm, w_a_out, g_q_norm, w_q_up, g_kv_norm, w_kv_up, w_b_out, w_o, w_up, w_down):
    depth = w_in.shape[0]
    assert depth == 1
    batch, seq, d = x_prompt.shape
    nb, t_new, _ = x_sample.shape
    past_len = page_table.shape[1] * PAGE_SIZE
    hk = A_HEADS * A_DK

    wi = w_in[0]
    o_qd = 3 * hk + A_HEADS * A_DV
    o_kpe = o_qd + Q_RANK + KV_RANK
    o_gate = o_kpe + ROPE_DIM
    w_main = jnp.concatenate([wi[:, :o_qd], wi[:, o_gate:]], axis=1).astype(BF)
    w_kpe = wi[:, o_kpe:o_gate]
    w_mla = jnp.concatenate([wi[:, o_qd:o_kpe], _pad_lanes(w_kpe), _pad_lanes(_swap_halves(w_kpe))],
                            axis=1).astype(BF)
    wq = w_q_up[0].reshape(Q_RANK, B_HEADS, NOPE_DIM + ROPE_DIM)
    wq_pe = wq[..., NOPE_DIM:]
    wq_cat = jnp.concatenate([wq[..., :NOPE_DIM].reshape(Q_RANK, -1),
                              _pad_lanes(wq_pe).reshape(Q_RANK, -1),
                              _pad_lanes(_swap_halves(wq_pe)).reshape(Q_RANK, -1)], axis=1).astype(BF)
    wkv = w_kv_up[0].reshape(KV_RANK, B_HEADS, NOPE_DIM + V_DIM)
    wuk = wkv[..., :NOPE_DIM].transpose(1, 2, 0).astype(BF)
    wuv = wkv[..., NOPE_DIM:].transpose(1, 0, 2).astype(BF)
    wuv_all = wkv[..., NOPE_DIM:].reshape(KV_RANK, B_HEADS * V_DIM).astype(BF)
    wa = w_a_out[0].astype(BF)
    wb = w_b_out[0].astype(BF)
    wo = w_o[0].astype(BF)
    wu = w_up[0].astype(BF)
    wd = w_down[0].astype(BF)

    c_all = jnp.concatenate([c_prompt, c_sample], axis=0)
    mod = _ada(c_all, w_ada[0], b_ada[0][None, :])
    mod = mod.reshape(batch + nb, 1, 6 * d)
    mod_p, mod_s = mod[:batch], mod[batch:]

    cos_p, sin_p = _rope_tables(jnp.arange(seq))
    cos_s, sin_s = _rope_tables(past_len + jnp.arange(t_new))
    gs = _TILE
    reps = gs
    cos_s = jnp.tile(cos_s, (reps, 1))
    sin_s = jnp.tile(sin_s, (reps, 1))

    def layer(x, mod_g, gb, tr, hgrn, attend, cos, sin, n_pos_tiles, act_dtype):
        tm = gb * tr
        proj = _proj(x, mod_g, g_pre_mix, w_main, gb, tr, 1024)
        proj_mla = _proj(x, mod_g, g_pre_mix, w_mla, gb, tr, w_mla.shape[1])
        o_a, s_fin = hgrn(proj)
        q_cat, kv_cat, ckv, kpe = _mla_prep(proj_mla, cos, sin, g_q_norm, g_kv_norm, wq_cat, wuk,
                                            tm, n_pos_tiles, act_dtype)
        o_b = attend(q_cat, kv_cat)
        mix_gb, mix_tr = (1, tr // 2) if gb == 1 else (gb // 2, tr)
        x1 = _mix(o_a.reshape(-1, o_a.shape[-1]), o_b.reshape(-1, o_b.shape[-1]), proj, x, mod_g,
                  g_post_mix, wa, wb, wo, mix_gb, mix_tr)
        y = _mlp(x1, mod_g, g_pre_mlp, g_post_mlp, wu, wd, gb, tr)
        return y, ckv, kpe, s_fin

    tr_p = 512
    y_p, ckv_p, kpe_p, s_p = layer(
        x_prompt, mod_p, 1, tr_p,
        lambda proj: _hgrn_prompt(proj, lb_logits, g_hgrn_norm, batch, seq),
        lambda q, kv: _attn_prompt(q, kv, wuv, batch, seq),
        cos_p, sin_p, seq // tr_p, BF)
    y_s, ckv_s, kpe_s, s_s = layer(
        x_sample, mod_s, gs, t_new,
        lambda proj: _hgrn_sample(proj, lb_logits, g_hgrn_norm, state_hgrn, nb, t_new),
        lambda q, kv: _attn_sample(page_table, q, kv, cache_ckv[0], cache_kpe[0], wuv_all, nb, t_new),
        cos_s, sin_s, 1, F32)

    return (y_p, y_s,
            ckv_p.reshape(1, batch, seq, KV_RANK), kpe_p.reshape(1, batch, seq, ROPE_DIM),
            s_p[None],
            ckv_s.reshape(1, nb, t_new, KV_RANK), kpe_s.reshape(1, nb, t_new, ROPE_DIM),
            s_s)
```

```python
import functools
import math

import numpy as np
import jax
import jax.numpy as jnp
from jax import lax
from jax.experimental import pallas as pl
from jax.experimental.pallas import tpu as pltpu

BF = jnp.bfloat16
F32 = jnp.float32

D_MODEL = 2048
A_HEADS = 8
A_DK = 128
A_DV = 128
A_CHUNK = 64
B_HEADS = 8
Q_RANK = 512
KV_RANK = 512
NOPE_DIM = 128
ROPE_DIM = 64
V_DIM = 128
ROPE_THETA = 10000.0
MLA_SCALE = (NOPE_DIM + ROPE_DIM) ** -0.5
D_FF = 4 * D_MODEL
EPS = 1e-6
PAGE_SIZE = 128

LANES = 128
QK_DIM = KV_RANK + LANES
VMEM_LIMIT = 56 * 1024 * 1024

_NT = (((1,), (1,)), ((), ()))
_TN = (((0,), (0,)), ((), ()))


def _dot(a, b):
    return jnp.dot(a, b, preferred_element_type=F32)


def _dot_nt(a, b):
    return lax.dot_general(a, b, _NT, preferred_element_type=F32)


def _dot_tn(a, b):
    return lax.dot_general(a, b, _TN, preferred_element_type=F32)


def _rms(x, g):
    r = lax.rsqrt(jnp.mean(x * x, axis=-1, keepdims=True) + EPS)
    return (x * r) * g


def _params(sem):
    return pltpu.CompilerParams(dimension_semantics=sem, vmem_limit_bytes=VMEM_LIMIT)


def _ada_body(c_ref, w_ref, b_ref, o_ref):
    c = c_ref[...]
    s = (c * jax.nn.sigmoid(c)).astype(BF)
    o_ref[...] = _dot(s, w_ref[...].astype(BF)) + b_ref[...]


def _ada(c_all, w_ada, b_ada, tn=1024):
    m, d = c_all.shape
    n = w_ada.shape[1]
    return pl.pallas_call(
        _ada_body,
        grid=(n // tn,),
        in_specs=[pl.BlockSpec((m, d), lambda j: (0, 0)),
                  pl.BlockSpec((d, tn), lambda j: (0, j)),
                  pl.BlockSpec((1, tn), lambda j: (0, j))],
        out_specs=pl.BlockSpec((m, tn), lambda j: (0, j)),
        out_shape=jax.ShapeDtypeStruct((m, n), F32),
        compiler_params=_params(("arbitrary",)),
        name="ada",
    )(c_all, w_ada, b_ada)


def _proj_body(x_ref, sh_ref, sc_ref, g_ref, w_ref, o_ref, h_ref):
    @pl.when(pl.program_id(2) == 0)
    def _():
        x = x_ref[...]
        h = _rms(x, g_ref[...]) * (1.0 + sc_ref[...]) + sh_ref[...]
        h_ref[...] = h.reshape(h_ref.shape).astype(BF)

    o_ref[...] = _dot(h_ref[...], w_ref[...])


def _proj(x, mod, g, w, gb, tr, tn):
    bm, r, d = x.shape
    n = w.shape[1]
    tm = gb * tr
    nr = r // tr
    return pl.pallas_call(
        _proj_body,
        grid=(bm // gb, nr, n // tn),
        in_specs=[pl.BlockSpec((gb, tr, d), lambda a, i, j: (a, i, 0)),
                  pl.BlockSpec((gb, 1, d), lambda a, i, j: (a, 0, 0)),
                  pl.BlockSpec((gb, 1, d), lambda a, i, j: (a, 0, 1)),
                  pl.BlockSpec((1, d), lambda a, i, j: (0, 0)),
                  pl.BlockSpec((d, tn), lambda a, i, j: (0, j))],
        out_specs=pl.BlockSpec((tm, tn), lambda a, i, j: (a * nr + i, j)),
        out_shape=jax.ShapeDtypeStruct((bm * r, n), F32),
        scratch_shapes=[pltpu.VMEM((tm, d), BF)],
        compiler_params=_params(("arbitrary", "arbitrary", "arbitrary")),
        name="proj",
    )(x, mod, mod, g, w)


_TILE = 64


def _hgrn_consts(chunk):
    t = np.arange(_TILE)[:, None]
    u = np.arange(_TILE)[None, :]
    same_chunk = (t // chunk) == (u // chunk)
    mats = [same_chunk & (u <= t),
            same_chunk & (u > t)]
    masks = []
    m = chunk
    while m >= 2:
        h = m // 2
        ref = (t // m) * m + h
        lower = (t % m) >= h
        mats.append(np.where(lower, (u > ref) & (u <= t), (u > t) & (u <= ref)))
        masks.append(((t // m) == (u // m)) & lower & ((u % m) < h))
        m = h
    masks.append(t == u)
    nm = np.concatenate([a.astype(np.float32) for a in mats], axis=0)
    mk = np.concatenate([a.astype(np.float32) for a in masks], axis=0)
    return jnp.asarray(nm, BF), jnp.asarray(mk, F32)


def _split3(x):
    hi = x.astype(BF)
    r1 = x - hi.astype(F32)
    mid = r1.astype(BF)
    lo = (r1 - mid.astype(F32)).astype(BF)
    return hi, mid, lo


def _hgrn_gates(qa, fa, lb):
    q = qa * jax.nn.sigmoid(qa)
    f = lb + (1.0 - lb) * jax.nn.sigmoid(fa)
    return q, jnp.log(f), 1.0 - f


def _hgrn_lb(lbl):
    e = jnp.exp(lbl - jnp.max(lbl, axis=0, keepdims=True))
    return e[0:1] / jnp.sum(e, axis=0, keepdims=True)


def _hgrn_scores(q, k, e_all, mk_ref, nlev):
    sc = jnp.where(mk_ref[nlev * _TILE:(nlev + 1) * _TILE, :] > 0.5,
                   _dot_nt(q.astype(BF), k.astype(BF)), 0.0)
    for l in range(nlev):
        e = e_all[(2 + l) * _TILE:(3 + l) * _TILE]
        p = _dot_nt((q * e).astype(BF), (k * e).astype(BF))
        sc = sc + jnp.where(mk_ref[l * _TILE:(l + 1) * _TILE, :] > 0.5, p, 0.0)
    return sc


def _hgrn_exps(nm_ref, logf):
    hi, mid, lo = _split3(logf)
    nm = nm_ref[...]
    return jnp.exp(_dot(nm, hi) + _dot(nm, mid) + _dot(nm, lo))


def _hgrn_out(o, ga, gn):
    return (_rms(o, gn) * jax.nn.sigmoid(ga)).astype(BF)


def _hgrn_p_body(qa_ref, fa_ref, ia_ref, ga_ref, lb_ref, gn_ref, nm_ref, mk_ref,
                 o_ref, s_ref, st_ref, *, nchunks, nlev, nh):
    ti = pl.program_id(2)

    @pl.when(ti == 0)
    def _():
        st_ref[...] = jnp.zeros_like(st_ref)

    lb = _hgrn_lb(lb_ref[...])
    gn = gn_ref[...]

    def chunk(c, carry):
        rows = pl.ds(pl.multiple_of(c * _TILE, _TILE), _TILE)
        qa, fa, ia, ga = (r[0, rows, :] for r in (qa_ref, fa_ref, ia_ref, ga_ref))
        sts = [st_ref[j] for j in range(nh)]
        hs = range(nh)
        cols = [slice(j * A_DK, (j + 1) * A_DK) for j in hs]
        gates = [_hgrn_gates(qa[:, c], fa[:, c], lb[:, c]) for c in cols]
        vs = [ia[:, c].astype(BF) for c in cols]
        es = [_hgrn_exps(nm_ref, logf) for _, logf, _ in gates]
        o_inter = [_dot_nt((gates[j][0] * es[j][0:_TILE]).astype(BF), sts[j].astype(BF)) for j in hs]
        scs = [_hgrn_scores(gates[j][0], gates[j][2], es[j], mk_ref, nlev) for j in hs]
        os_ = [o_inter[j] + _dot(scs[j].astype(BF), vs[j]) for j in hs]
        upd = [_dot_tn(vs[j], (gates[j][2] * es[j][_TILE:2 * _TILE]).astype(BF)) for j in hs]
        for j in hs:
            st_ref[j] = sts[j] * es[j][_TILE - 1:_TILE, :] + upd[j]
        o_ref[0, rows, :] = jnp.concatenate(
            [_hgrn_out(os_[j], ga[:, cols[j]], gn[:, cols[j]]) for j in hs], axis=1)
        return carry

    lax.fori_loop(0, nchunks, chunk, 0)

    @pl.when(ti == pl.num_programs(2) - 1)
    def _():
        for j in range(nh):
            s_ref[0, j] = st_ref[j].T


def _hgrn_prompt(proj, lb_logits, g_norm, b, t, tt=512, nh=4):
    nm, mk = _hgrn_consts(A_CHUNK)
    nlev = int(math.log2(A_CHUNK))
    h = A_HEADS
    hg = h // nh
    proj3 = proj.reshape(b, t, proj.shape[-1])

    def col(off):
        return pl.BlockSpec((1, tt, nh * A_DK), lambda bi, hi, ti: (bi, ti, off + hi))

    body = functools.partial(_hgrn_p_body, nchunks=tt // _TILE, nlev=nlev, nh=nh)
    return pl.pallas_call(
        body,
        grid=(b, hg, t // tt),
        in_specs=[col(0), col(hg), col(2 * hg), col(3 * hg),
                  pl.BlockSpec((lb_logits.shape[0], nh * A_DK), lambda bi, hi, ti: (0, hi)),
                  pl.BlockSpec((1, nh * A_DV), lambda bi, hi, ti: (0, hi)),
                  pl.BlockSpec(nm.shape, lambda bi, hi, ti: (0, 0)),
                  pl.BlockSpec(mk.shape, lambda bi, hi, ti: (0, 0))],
        out_specs=[pl.BlockSpec((1, tt, nh * A_DV), lambda bi, hi, ti: (bi, ti, hi)),
                   pl.BlockSpec((1, nh, A_DK, A_DV), lambda bi, hi, ti: (bi, hi, 0, 0))],
        out_shape=[jax.ShapeDtypeStruct((b, t, h * A_DV), BF),
                   jax.ShapeDtypeStruct((b, h, A_DK, A_DV), F32)],
        scratch_shapes=[pltpu.VMEM((nh, A_DV, A_DK), F32)],
        compiler_params=_params(("arbitrary", "arbitrary", "arbitrary")),
        name="hgrn_prompt",
    )(proj3, proj3, proj3, proj3, lb_logits, g_norm, nm, mk)


def _hgrn_s_body(qa_ref, fa_ref, ia_ref, ga_ref, lb_ref, gn_ref, nm_ref, mk_ref, sel_ref,
                 s0_ref, o_ref, s_ref, *, nlev, gb):
    lb = _hgrn_lb(lb_ref[...])
    q, logf, k = _hgrn_gates(qa_ref[...], fa_ref[...], lb)
    vf = ia_ref[...]
    v = vf.astype(BF)
    e_all = _hgrn_exps(nm_ref, logf)
    eb = e_all[0:_TILE]
    er = e_all[_TILE:2 * _TILE]
    sel = sel_ref[...]
    hi, mid, lo = _split3(logf)
    selb = sel.astype(BF)
    dec = jnp.exp(_dot_tn(hi, selb) + _dot_tn(mid, selb) + _dot_tn(lo, selb))
    qe = q * eb
    q_blk = (jnp.concatenate([qe] * gb, axis=1) * sel).astype(BF)
    v_blk = (jnp.concatenate([vf] * gb, axis=1) * sel).astype(BF)
    s0 = s0_ref[0, :, 0]
    o = _dot(q_blk, s0.reshape(gb * A_DK, A_DV).astype(BF))
    sc = _hgrn_scores(q, k, e_all, mk_ref, nlev)
    o = o + _dot(sc.astype(BF), v)
    upd = _dot_tn((k * er).astype(BF), v_blk)
    for j in range(gb):
        cols = slice(j * A_DV, (j + 1) * A_DV)
        s_ref[0, j, 0] = dec[:, cols] * s0[j] + upd[:, cols]
    o_ref[...] = _hgrn_out(o, ga_ref[...], gn_ref[...])


def _hgrn_sample(proj, lb_logits, g_norm, state, nb, t):
    gb = _TILE // t
    nm, mk = _hgrn_consts(t)
    nlev = int(math.log2(t))
    h = A_HEADS
    sel = np.zeros((_TILE, gb * A_DV), np.float32)
    for j in range(gb):
        sel[j * t:(j + 1) * t, j * A_DV:(j + 1) * A_DV] = 1.0
    sel = jnp.asarray(sel)

    def col(off):
        return pl.BlockSpec((_TILE, A_DK), lambda i, hi: (i, off + hi))

    body = functools.partial(_hgrn_s_body, nlev=nlev, gb=gb)
    st_spec = pl.BlockSpec((1, gb, 1, A_DK, A_DV), lambda i, hi: (0, i, hi, 0, 0))
    return pl.pallas_call(
        body,
        grid=(nb // gb, h),
        in_specs=[col(0), col(h), col(2 * h), col(3 * h),
                  pl.BlockSpec((lb_logits.shape[0], A_DK), lambda i, hi: (0, hi)),
                  pl.BlockSpec((1, A_DV), lambda i, hi: (0, hi)),
                  pl.BlockSpec(nm.shape, lambda i, hi: (0, 0)),
                  pl.BlockSpec(mk.shape, lambda i, hi: (0, 0)),
                  pl.BlockSpec(sel.shape, lambda i, hi: (0, 0)),
                  st_spec],
        out_specs=[pl.BlockSpec((_TILE, A_DV), lambda i, hi: (i, hi)), st_spec],
        out_shape=[jax.ShapeDtypeStruct((nb * t, h * A_DV), BF),
                   jax.ShapeDtypeStruct(state.shape, F32)],
        compiler_params=_params(("arbitrary", "arbitrary")),
        name="hgrn_sample",
    )(proj, proj, proj, proj, lb_logits, g_norm, nm, mk, sel, state)


def _mla_prep_body(p_ref, cos_ref, sin_ref, gq_ref, gkv_ref, wq_ref, wuk_ref,
                   q_ref, kv_ref, ckv_ref, kpe_ref):
    p = p_ref[...]
    cos = cos_ref[...]
    sin = sin_ref[...]
    qn = _rms(p[:, :Q_RANK], gq_ref[...]).astype(BF)
    qf = _dot(qn, wq_ref[...])
    hw = B_HEADS * NOPE_DIM
    for h in range(B_HEADS):
        c0 = h * NOPE_DIM
        q_lat = _dot(qf[:, c0:c0 + NOPE_DIM].astype(BF), wuk_ref[h])
        q_pe = qf[:, hw + c0:hw + c0 + LANES] * cos + qf[:, 2 * hw + c0:2 * hw + c0 + LANES] * sin
        q_ref[:, h * QK_DIM:h * QK_DIM + KV_RANK] = q_lat.astype(q_ref.dtype)
        q_ref[:, h * QK_DIM + KV_RANK:(h + 1) * QK_DIM] = q_pe.astype(q_ref.dtype)
    ckv = _rms(p[:, Q_RANK:Q_RANK + KV_RANK], gkv_ref[...])
    o = Q_RANK + KV_RANK
    kpe = p[:, o:o + LANES] * cos + p[:, o + LANES:o + 2 * LANES] * sin
    ckv_ref[...] = ckv
    kpe_ref[...] = kpe[:, :ROPE_DIM]
    kv_ref[:, :KV_RANK] = ckv.astype(kv_ref.dtype)
    kv_ref[:, KV_RANK:] = kpe.astype(kv_ref.dtype)


def _mla_prep(proj_mla, cos, sin, g_q, g_kv, wq, wuk, tm, n_pos_tiles, act_dtype):
    rows, pc = proj_mla.shape
    return pl.pallas_call(
        _mla_prep_body,
        grid=(rows // tm,),
        in_specs=[pl.BlockSpec((tm, pc), lambda i: (i, 0)),
                  pl.BlockSpec((tm, LANES), lambda i: (i % n_pos_tiles, 0)),
                  pl.BlockSpec((tm, LANES), lambda i: (i % n_pos_tiles, 0)),
                  pl.BlockSpec((1, Q_RANK), lambda i: (0, 0)),
                  pl.BlockSpec((1, KV_RANK), lambda i: (0, 0)),
                  pl.BlockSpec(wq.shape, lambda i: (0, 0)),
                  pl.BlockSpec(wuk.shape, lambda i: (0, 0, 0))],
        out_specs=[pl.BlockSpec((tm, B_HEADS * QK_DIM), lambda i: (i, 0)),
                   pl.BlockSpec((tm, QK_DIM), lambda i: (i, 0)),
                   pl.BlockSpec((tm, KV_RANK), lambda i: (i, 0)),
                   pl.BlockSpec((tm, ROPE_DIM), lambda i: (i, 0))],
        out_shape=[jax.ShapeDtypeStruct((rows, B_HEADS * QK_DIM), act_dtype),
                   jax.ShapeDtypeStruct((rows, QK_DIM), act_dtype),
                   jax.ShapeDtypeStruct((rows, KV_RANK), F32),
                   jax.ShapeDtypeStruct((rows, ROPE_DIM), F32)],
        compiler_params=_params(("arbitrary",)),
        name="mla_prep",
    )(proj_mla, cos, sin, g_q, g_kv, wq, wuk)


_EXP2_SCALE = MLA_SCALE * math.log2(math.e)


def _lane_blocks(x):
    return [x[:, k * LANES:(k + 1) * LANES] for k in range(x.shape[1] // LANES)]


def _softmax_step(s, v, m_ref, l_ref, acc_ref):
    blocks = _lane_blocks(s)
    m_prev = m_ref[...]
    m_new = jnp.maximum(m_prev, jnp.max(functools.reduce(jnp.maximum, blocks), axis=-1, keepdims=True))
    m_ref[...] = m_new
    alpha = jnp.exp2((m_prev - m_new) * _EXP2_SCALE)
    ps = [jnp.exp2((blk - m_new) * _EXP2_SCALE) for blk in blocks]
    l_ref[...] = alpha * l_ref[...] + functools.reduce(jnp.add, ps)
    pv = _dot(jnp.concatenate(ps, axis=1).astype(BF), v)
    for k, blk in enumerate(_lane_blocks(pv)):
        cols = slice(k * LANES, (k + 1) * LANES)
        acc_ref[:, cols] = alpha * acc_ref[:, cols] + blk


def _softmax_finish(acc, l_part):
    return acc / jnp.sum(l_part, axis=-1, keepdims=True)


def _attn_p_body(q_ref, kv_ref, wuv_ref, o_ref, *scratch, tq, hpar):
    qi = pl.program_id(1)
    row = lax.broadcasted_iota(jnp.int32, (tq, tq), 0)
    col = lax.broadcasted_iota(jnp.int32, (tq, tq), 1)
    stats = [scratch[3 * i:3 * i + 3] for i in range(hpar)]
    for h0 in range(0, B_HEADS, hpar):
        qs = [q_ref[0, :, (h0 + i) * QK_DIM:(h0 + i + 1) * QK_DIM] for i in range(hpar)]
        for acc_ref, m_ref, l_ref in stats:
            m_ref[...] = jnp.full_like(m_ref, -jnp.inf)
            l_ref[...] = jnp.zeros_like(l_ref)
            acc_ref[...] = jnp.zeros_like(acc_ref)

        def block(j, diagonal):
            kv = kv_ref[0, pl.ds(pl.multiple_of(j * tq, tq), tq), :]
            ss = [_dot_nt(qh, kv) for qh in qs]
            for s, (acc_ref, m_ref, l_ref) in zip(ss, stats):
                if diagonal:
                    s = jnp.where(col <= row, s, -jnp.inf)
                _softmax_step(s, kv[:, :KV_RANK], m_ref, l_ref, acc_ref)

        def body(j, carry):
            block(j, False)
            return carry

        lax.fori_loop(0, qi, body, 0)
        block(qi, True)
        for i, (acc_ref, m_ref, l_ref) in enumerate(stats):
            h = h0 + i
            o_lat = _softmax_finish(acc_ref[...], l_ref[...]).astype(BF)
            o_ref[0, :, h * V_DIM:(h + 1) * V_DIM] = _dot(o_lat, wuv_ref[h]).astype(o_ref.dtype)


def _attn_prompt(q_cat, kv_cat, wuv, b, t, tq=512, hpar=2):
    q3 = q_cat.reshape(b, t, q_cat.shape[-1])
    kv3 = kv_cat.reshape(b, t, kv_cat.shape[-1])
    body = functools.partial(_attn_p_body, tq=tq, hpar=hpar)
    return pl.pallas_call(
        body,
        grid=(b, t // tq),
        in_specs=[pl.BlockSpec((1, tq, q3.shape[-1]), lambda bi, qi: (bi, qi, 0)),
                  pl.BlockSpec((1, t, QK_DIM), lambda bi, qi: (bi, 0, 0)),
                  pl.BlockSpec(wuv.shape, lambda bi, qi: (0, 0, 0))],
        out_specs=pl.BlockSpec((1, tq, B_HEADS * V_DIM), lambda bi, qi: (bi, qi, 0)),
        out_shape=jax.ShapeDtypeStruct((b, t, B_HEADS * V_DIM), BF),
        scratch_shapes=[pltpu.VMEM((tq, KV_RANK), F32),
                        pltpu.VMEM((tq, LANES), F32),
                        pltpu.VMEM((tq, LANES), F32)] * hpar,
        compiler_params=_params(("arbitrary", "arbitrary")),
        name="attn_prompt",
    )(q3, kv3, wuv)


def _attn_s_body(pt_ref, q_ref, kvn_ref, ckv_hbm, kpe_hbm, wuv_ref, o_ref,
                 ckv_buf, kpe_buf, sem, qs_ref, acc_ref, m_ref, l_ref,
                 *, nch, ppc, n_pages, t_new):
    b = pl.program_id(0)
    c = pl.program_id(1)
    n = b * nch + c
    total = pl.num_programs(0) * nch
    slot = n % 2

    def page_copies(step, slot_, p):
        page = pt_ref[(step // nch) * n_pages + (step % nch) * ppc + p]
        keys = pl.ds(p * PAGE_SIZE, PAGE_SIZE)
        return (pltpu.make_async_copy(ckv_hbm.at[page], ckv_buf.at[slot_, keys, :], sem.at[slot_, 0]),
                pltpu.make_async_copy(kpe_hbm.at[page], kpe_buf.at[slot_, :, keys], sem.at[slot_, 1]))

    def start_all(step, slot_):
        for p in range(ppc):
            for cp in page_copies(step, slot_, p):
                cp.start()

    @pl.when(n == 0)
    def _():
        start_all(n, slot)

    @pl.when(n + 1 < total)
    def _():
        start_all(n + 1, 1 - slot)

    @pl.when(c == 0)
    def _():
        for h in range(B_HEADS):
            qs_ref[h * t_new:(h + 1) * t_new, :] = q_ref[0, :, h * QK_DIM:(h + 1) * QK_DIM]
        m_ref[...] = jnp.full_like(m_ref, -jnp.inf)
        l_ref[...] = jnp.zeros_like(l_ref)
        acc_ref[...] = jnp.zeros_like(acc_ref)

    for p in range(ppc):
        for cp in page_copies(n, slot, p):
            cp.wait()

    qs = qs_ref[...]
    qb = qs.astype(BF)
    kc = ckv_buf[slot].astype(BF)
    kpt = kpe_buf[slot].astype(BF)
    s = _dot_nt(qb[:, :KV_RANK], kc) + _dot(qb[:, KV_RANK:KV_RANK + ROPE_DIM], kpt)
    _softmax_step(s, kc, m_ref, l_ref, acc_ref)

    @pl.when(c == nch - 1)
    def _():
        kn = kvn_ref[0]
        rows = B_HEADS * t_new
        sn = _dot_nt(qs, kn)
        tq_ = lax.broadcasted_iota(jnp.int32, (rows, t_new), 0) % t_new
        tk_ = lax.broadcasted_iota(jnp.int32, (rows, t_new), 1)
        sn = jnp.where(tk_ <= tq_, sn, -jnp.inf)
        m_prev = m_ref[:, 0:1]
        m_new = jnp.maximum(m_prev, jnp.max(sn, axis=-1, keepdims=True))
        alpha = jnp.exp2((m_prev - m_new) * _EXP2_SCALE)
        pn = jnp.exp2((sn - m_new) * _EXP2_SCALE)
        l = alpha * jnp.sum(l_ref[...], axis=-1, keepdims=True) + jnp.sum(pn, axis=-1, keepdims=True)
        acc = alpha * acc_ref[...]
        vn = kn[:, :KV_RANK].astype(BF).astype(F32)
        pn = pn.astype(BF).astype(F32)
        for j in range(t_new):
            acc = acc + pn[:, j:j + 1] * vn[j:j + 1, :]
        o_lat = (acc / l).astype(BF)
        r = _dot(o_lat, wuv_ref[...])
        for h in range(B_HEADS):
            o_ref[0, :, h * V_DIM:(h + 1) * V_DIM] = r[h * t_new:(h + 1) * t_new, h * V_DIM:(h + 1) * V_DIM]


def _attn_sample(page_table, q_cat, kv_cat, cache_ckv, cache_kpe, wuv_all, nb, t_new, ppc=16):
    n_pages = page_table.shape[1]
    nch = n_pages // ppc
    keys = ppc * PAGE_SIZE
    q3 = q_cat.reshape(nb, t_new, q_cat.shape[-1])
    kv3 = kv_cat.reshape(nb, t_new, kv_cat.shape[-1])
    rows = B_HEADS * t_new
    body = functools.partial(_attn_s_body, nch=nch, ppc=ppc, n_pages=n_pages, t_new=t_new)
    grid_spec = pltpu.PrefetchScalarGridSpec(
        num_scalar_prefetch=1,
        grid=(nb, nch),
        in_specs=[pl.BlockSpec((1, t_new, q3.shape[-1]), lambda bi, ci, pt: (bi, 0, 0)),
                  pl.BlockSpec((1, t_new, QK_DIM), lambda bi, ci, pt: (bi, 0, 0)),
                  pl.BlockSpec(memory_space=pl.ANY),
                  pl.BlockSpec(memory_space=pl.ANY),
                  pl.BlockSpec(wuv_all.shape, lambda bi, ci, pt: (0, 0))],
        out_specs=pl.BlockSpec((1, t_new, B_HEADS * V_DIM), lambda bi, ci, pt: (bi, 0, 0)),
        scratch_shapes=[pltpu.VMEM((2, keys, KV_RANK), F32),
                        pltpu.VMEM((2, ROPE_DIM, keys), F32),
                        pltpu.SemaphoreType.DMA((2, 2)),
                        pltpu.VMEM((rows, QK_DIM), F32),
                        pltpu.VMEM((rows, KV_RANK), F32),
                        pltpu.VMEM((rows, LANES), F32),
                        pltpu.VMEM((rows, LANES), F32)])
    return pl.pallas_call(
        body,
        grid_spec=grid_spec,
        out_shape=jax.ShapeDtypeStruct((nb, t_new, B_HEADS * V_DIM), F32),
        compiler_params=_params(("arbitrary", "arbitrary")),
        name="attn_sample",
    )(page_table.reshape(-1), q3, kv3, cache_ckv, cache_kpe, wuv_all)


def _mix_body(oa_ref, ob_ref, ga_ref, gb_ref, x_ref, gt_ref, g_ref, wa_ref, wb_ref, wo_ref, o_ref):
    ya = _dot(oa_ref[...].astype(BF), wa_ref[...])
    yb = _dot(ob_ref[...].astype(BF), wb_ref[...])
    merged = jax.nn.sigmoid(ga_ref[...]) * ya + jax.nn.sigmoid(gb_ref[...]) * yb
    z = _dot(merged.astype(BF), wo_ref[...])
    x = x_ref[...]
    o_ref[...] = x + gt_ref[...] * _rms(z, g_ref[...]).reshape(x.shape)


def _mix(o_a, o_b, proj, x, mod, g, wa, wb, wo, gb, tr):
    bm, r, d = x.shape
    tm = gb * tr
    nr = r // tr
    gate0 = 4 * A_HEADS * A_DK // d

    def rows2(width, cb):
        return pl.BlockSpec((tm, width), lambda a, i: (a * nr + i, cb))

    def const(w):
        return pl.BlockSpec(w.shape, lambda a, i: (0, 0))

    return pl.pallas_call(
        _mix_body,
        grid=(bm // gb, nr),
        in_specs=[rows2(o_a.shape[1], 0), rows2(o_b.shape[1], 0),
                  rows2(d, gate0), rows2(d, gate0 + 1),
                  pl.BlockSpec((gb, tr, d), lambda a, i: (a, i, 0)),
                  pl.BlockSpec((gb, 1, d), lambda a, i: (a, 0, 2)),
                  const(g), const(wa), const(wb), const(wo)],
        out_specs=pl.BlockSpec((gb, tr, d), lambda a, i: (a, i, 0)),
        out_shape=jax.ShapeDtypeStruct(x.shape, F32),
        compiler_params=_params(("arbitrary", "arbitrary")),
        name="mix",
    )(o_a, o_b, proj, proj, x, mod, g, wa, wb, wo)


def _mlp_body(x_ref, sh_ref, sc_ref, gt_ref, g1_ref, g2_ref, wu_ref, wd_ref, o_ref, h_ref, acc_ref):
    f = pl.program_id(2)

    @pl.when(f == 0)
    def _():
        x = x_ref[...]
        h = _rms(x, g1_ref[...]) * (1.0 + sc_ref[...]) + sh_ref[...]
        h_ref[...] = h.reshape(h_ref.shape).astype(BF)
        acc_ref[...] = jnp.zeros_like(acc_ref)

    u = jnp.maximum(_dot(h_ref[...], wu_ref[...]), 0.0)
    acc_ref[...] += _dot((u * u).astype(BF), wd_ref[...])

    @pl.when(f == pl.num_programs(2) - 1)
    def _():
        x = x_ref[...]
        o_ref[...] = x + gt_ref[...] * _rms(acc_ref[...], g2_ref[...]).reshape(x.shape)


def _mlp(x, mod, g1, g2, wu, wd, gb, tr, tf=512):
    bm, r, d = x.shape
    tm = gb * tr
    nr = r // tr
    ff = wu.shape[1]

    def modspec(piece):
        return pl.BlockSpec((gb, 1, d), lambda a, i, f: (a, 0, piece))

    return pl.pallas_call(
        _mlp_body,
        grid=(bm // gb, nr, ff // tf),
        in_specs=[pl.BlockSpec((gb, tr, d), lambda a, i, f: (a, i, 0)),
                  modspec(3), modspec(4), modspec(5),
                  pl.BlockSpec((1, d), lambda a, i, f: (0, 0)),
                  pl.BlockSpec((1, d), lambda a, i, f: (0, 0)),
                  pl.BlockSpec((d, tf), lambda a, i, f: (0, f)),
                  pl.BlockSpec((tf, d), lambda a, i, f: (f, 0))],
        out_specs=pl.BlockSpec((gb, tr, d), lambda a, i, f: (a, i, 0)),
        out_shape=jax.ShapeDtypeStruct(x.shape, F32),
        scratch_shapes=[pltpu.VMEM((tm, d), BF), pltpu.VMEM((tm, d), F32)],
        compiler_params=_params(("arbitrary", "arbitrary", "arbitrary")),
        name="mlp",
    )(x, mod, mod, mod, g1, g2, wu, wd)


def _rope_tables(pos):
    half = ROPE_DIM // 2
    inv = ROPE_THETA ** (-jnp.arange(half, dtype=F32) / half)
    ang = pos.astype(F32)[:, None] * inv[None, :]
    z = jnp.zeros((pos.shape[0], LANES - ROPE_DIM), F32)
    cos = jnp.cos(ang)
    sin = jnp.sin(ang)
    return jnp.concatenate([cos, cos, z], axis=1), jnp.concatenate([sin, sin, z], axis=1)


def _swap_halves(w):
    half = w.shape[-1] // 2
    return jnp.concatenate([-w[..., half:], w[..., :half]], axis=-1)


def _pad_lanes(w):
    pad = [(0, 0)] * (w.ndim - 1) + [(0, LANES - w.shape[-1])]
    return jnp.pad(w, pad)


def kernel(x_prompt, x_sample, c_prompt, c_sample, cache_ckv, cache_kpe, state_hgrn, page_table, w_ada, b_ada, g_pre_mix, g_post_mix, g_pre_mlp, g_post_mlp, w_in, lb_logits, g_hgrn_norm, w_a_out, g_q_norm, w_q_up, g_kv_norm, w_kv_up, w_b_out, w_o, w_up, w_down):
    depth = w_in.shape[0]
    assert depth == 1
    batch, seq, d = x_prompt.shape
    nb, t_new, _ = x_sample.shape
    past_len = page_table.shape[1] * PAGE_SIZE
    hk = A_HEADS * A_DK

    wi = w_in[0]
    o_qd = 3 * hk + A_HEADS * A_DV
    o_kpe = o_qd + Q_RANK + KV_RANK
    o_gate = o_kpe + ROPE_DIM
    w_main = jnp.concatenate([wi[:, :o_qd], wi[:, o_gate:]], axis=1).astype(BF)
    w_kpe = wi[:, o_kpe:o_gate]
    w_mla = jnp.concatenate([wi[:, o_qd:o_kpe], _pad_lanes(w_kpe), _pad_lanes(_swap_halves(w_kpe))],
                            axis=1).astype(BF)
    wq = w_q_up[0].reshape(Q_RANK, B_HEADS, NOPE_DIM + ROPE_DIM)
    wq_pe = wq[..., NOPE_DIM:]
    wq_cat = jnp.concatenate([wq[..., :NOPE_DIM].reshape(Q_RANK, -1),
                              _pad_lanes(wq_pe).reshape(Q_RANK, -1),
                              _pad_lanes(_swap_halves(wq_pe)).reshape(Q_RANK, -1)], axis=1).astype(BF)
    wkv = w_kv_up[0].reshape(KV_RANK, B_HEADS, NOPE_DIM + V_DIM)
    wuk = wkv[..., :NOPE_DIM].transpose(1, 2, 0).astype(BF)
    wuv = wkv[..., NOPE_DIM:].transpose(1, 0, 2).astype(BF)
    wuv_all = wkv[..., NOPE_DIM:].reshape(KV_RANK, B_HEADS * V_DIM).astype(BF)
    wa = w_a_out[0].astype(BF)
    wb = w_b_out[0].astype(BF)
    wo = w_o[0].astype(BF)
    wu = w_up[0].astype(BF)
    wd = w_down[0].astype(BF)

    c_all = jnp.concatenate([c_prompt, c_sample], axis=0)
    mod = _ada(c_all, w_ada[0], b_ada[0][None, :])
    mod = mod.reshape(batch + nb, 1, 6 * d)
    mod_p, mod_s = mod[:batch], mod[batch:]

    cos_p, sin_p = _rope_tables(jnp.arange(seq))
    cos_s, sin_s = _rope_tables(past_len + jnp.arange(t_new))
    gs = _TILE
    reps = gs
    cos_s = jnp.tile(cos_s, (reps, 1))
    sin_s = jnp.tile(sin_s, (reps, 1))

    def layer(x, mod_g, gb, tr, hgrn, attend, cos, sin, n_pos_tiles, act_dtype):
        tm = gb * tr
        proj = _proj(x, mod_g, g_pre_mix, w_main, gb, tr, 1024)
        proj_mla = _proj(x, mod_g, g_pre_mix, w_mla, gb, tr, w_mla.shape[1])
        o_a, s_fin = hgrn(proj)
        q_cat, kv_cat, ckv, kpe = _mla_prep(proj_mla, cos, sin, g_q_norm, g_kv_norm, wq_cat, wuk,
                                            tm, n_pos_tiles, act_dtype)
        o_b = attend(q_cat, kv_cat)
        mix_gb, mix_tr = (1, tr // 2) if gb == 1 else (gb // 2, tr)
        x1 = _mix(o_a.reshape(-1, o_a.shape[-1]), o_b.reshape(-1, o_b.shape[-1]), proj, x, mod_g,
                  g_post_mix, wa, wb, wo, mix_gb, mix_tr)
        y = _mlp(x1, mod_g, g_pre_mlp, g_post_mlp, wu, wd, gb, tr)
        return y, ckv, kpe, s_fin

    kpe_pages_t = jnp.swapaxes(cache_kpe[0], 1, 2)

    tr_p = 512
    y_p, ckv_p, kpe_p, s_p = layer(
        x_prompt, mod_p, 1, tr_p,
        lambda proj: _hgrn_prompt(proj, lb_logits, g_hgrn_norm, batch, seq),
        lambda q, kv: _attn_prompt(q, kv, wuv, batch, seq),
        cos_p, sin_p, seq // tr_p, BF)
    y_s, ckv_s, kpe_s, s_s = layer(
        x_sample, mod_s, gs, t_new,
        lambda proj: _hgrn_sample(proj, lb_logits, g_hgrn_norm, state_hgrn, nb, t_new),
        lambda q, kv: _attn_sample(page_table, q, kv, cache_ckv[0], kpe_pages_t, wuv_all, nb, t_new),
        cos_s, sin_s, 1, F32)

    return (y_p, y_s,
            ckv_p.reshape(1, batch, seq, KV_RANK), kpe_p.reshape(1, batch, seq, ROPE_DIM),
            s_p[None],
            ckv_s.reshape(1, nb, t_new, KV_RANK), kpe_s.reshape(1, nb, t_new, ROPE_DIM),
            s_s)
```

```python
import functools
import math

import numpy as np
import jax
import jax.numpy as jnp
from jax import lax
from jax.experimental import pallas as pl
from jax.experimental.pallas import tpu as pltpu

BF = jnp.bfloat16
F32 = jnp.float32

D_MODEL = 2048
A_HEADS = 8
A_DK = 128
A_DV = 128
A_CHUNK = 64
B_HEADS = 8
Q_RANK = 512
KV_RANK = 512
NOPE_DIM = 128
ROPE_DIM = 64
V_DIM = 128
ROPE_THETA = 10000.0
MLA_SCALE = (NOPE_DIM + ROPE_DIM) ** -0.5
D_FF = 4 * D_MODEL
EPS = 1e-6
PAGE_SIZE = 128

LANES = 128
QK_DIM = KV_RANK + LANES
VMEM_LIMIT = 56 * 1024 * 1024

_NT = (((1,), (1,)), ((), ()))
_TN = (((0,), (0,)), ((), ()))


def _dot(a, b):
    return jnp.dot(a, b, preferred_element_type=F32)


def _dot_nt(a, b):
    return lax.dot_general(a, b, _NT, preferred_element_type=F32)


def _dot_tn(a, b):
    return lax.dot_general(a, b, _TN, preferred_element_type=F32)


def _rms(x, g):
    r = lax.rsqrt(jnp.mean(x * x, axis=-1, keepdims=True) + EPS)
    return (x * r) * g


def _params(sem):
    return pltpu.CompilerParams(dimension_semantics=sem, vmem_limit_bytes=VMEM_LIMIT)


def _ada_body(c_ref, w_ref, b_ref, o_ref):
    c = c_ref[...]
    s = (c * jax.nn.sigmoid(c)).astype(BF)
    o_ref[...] = _dot(s, w_ref[...].astype(BF)) + b_ref[...]


def _ada(c_all, w_ada, b_ada, tn=1024):
    m, d = c_all.shape
    n = w_ada.shape[1]
    return pl.pallas_call(
        _ada_body,
        grid=(n // tn,),
        in_specs=[pl.BlockSpec((m, d), lambda j: (0, 0)),
                  pl.BlockSpec((d, tn), lambda j: (0, j)),
                  pl.BlockSpec((1, tn), lambda j: (0, j))],
        out_specs=pl.BlockSpec((m, tn), lambda j: (0, j)),
        out_shape=jax.ShapeDtypeStruct((m, n), F32),
        compiler_params=_params(("arbitrary",)),
        name="ada",
    )(c_all, w_ada, b_ada)


def _proj_body(x_ref, sh_ref, sc_ref, g_ref, w_ref, o_ref, h_ref):
    @pl.when(pl.program_id(2) == 0)
    def _():
        x = x_ref[...]
        h = _rms(x, g_ref[...]) * (1.0 + sc_ref[...]) + sh_ref[...]
        h_ref[...] = h.reshape(h_ref.shape).astype(BF)

    o_ref[...] = _dot_nt(h_ref[...], w_ref[...])


def _proj(x, mod, g, w_t, gb, tr, tn):
    bm, r, d = x.shape
    n = w_t.shape[0]
    tm = gb * tr
    nr = r // tr
    return pl.pallas_call(
        _proj_body,
        grid=(bm // gb, nr, n // tn),
        in_specs=[pl.BlockSpec((gb, tr, d), lambda a, i, j: (a, i, 0)),
                  pl.BlockSpec((gb, 1, d), lambda a, i, j: (a, 0, 0)),
                  pl.BlockSpec((gb, 1, d), lambda a, i, j: (a, 0, 1)),
                  pl.BlockSpec((1, d), lambda a, i, j: (0, 0)),
                  pl.BlockSpec((tn, d), lambda a, i, j: (j, 0))],
        out_specs=pl.BlockSpec((tm, tn), lambda a, i, j: (a * nr + i, j)),
        out_shape=jax.ShapeDtypeStruct((bm * r, n), F32),
        scratch_shapes=[pltpu.VMEM((tm, d), BF)],
        compiler_params=_params(("arbitrary", "arbitrary", "arbitrary")),
        name="proj",
    )(x, mod, mod, g, w_t)


_TILE = 64


def _hgrn_consts(chunk):
    t = np.arange(_TILE)[:, None]
    u = np.arange(_TILE)[None, :]
    tril = ((t // chunk) == (u // chunk)) & (u <= t)
    masks = []
    m = chunk
    while m >= 2:
        h = m // 2
        masks.append(((t // m) == (u // m)) & ((t % m) >= h) & ((u % m) < h))
        m = h
    masks.append(t == u)
    mk = np.concatenate([a.astype(np.float32) for a in masks], axis=0)
    return jnp.asarray(tril.astype(np.float32), BF), jnp.asarray(mk, F32)


def _split3(x):
    hi = x.astype(BF)
    r1 = x - hi.astype(F32)
    mid = r1.astype(BF)
    lo = (r1 - mid.astype(F32)).astype(BF)
    return hi, mid, lo


def _hgrn_gates(qa, fa, lb):
    q = qa * jax.nn.sigmoid(qa)
    f = lb + (1.0 - lb) * jax.nn.sigmoid(fa)
    return q, jnp.log(f), 1.0 - f


def _hgrn_lb(lbl):
    e = jnp.exp(lbl - jnp.max(lbl, axis=0, keepdims=True))
    return e[0:1] / jnp.sum(e, axis=0, keepdims=True)


def _hgrn_scores(q, k, levels, mk_ref):
    nlev = len(levels)
    sc = jnp.where(mk_ref[nlev * _TILE:(nlev + 1) * _TILE, :] > 0.5,
                   _dot_nt(q.astype(BF), k.astype(BF)), 0.0)
    for l, e in enumerate(levels):
        p = _dot_nt((q * e).astype(BF), (k * e).astype(BF))
        sc = sc + jnp.where(mk_ref[l * _TILE:(l + 1) * _TILE, :] > 0.5, p, 0.0)
    return sc


def _row_bcast(x, rows, n):
    w = x.shape[1]
    return jnp.concatenate([jnp.broadcast_to(x[r:r + 1, :], (n, w)) for r in rows], axis=0)


def _hgrn_decays(tril_ref, logf, chunk):
    w = logf.shape[1]
    a = _dot(tril_ref[...], jnp.concatenate(_split3(logf), axis=1))
    b = a[:, :w] + a[:, w:2 * w] + a[:, 2 * w:]
    sub = lax.broadcasted_iota(jnp.int32, (8, w), 0)

    def refs_in_groups(offs):
        span = 8 // len(offs)
        pieces = []
        for g in range(_TILE // 8):
            rows = [jnp.broadcast_to(b[g * 8 + o:g * 8 + o + 1, :], (8, w)) for o in offs]
            ref = rows[-1]
            for i in range(len(offs) - 2, -1, -1):
                ref = jnp.where(sub < (i + 1) * span, rows[i], ref)
            pieces.append(ref)
        return jnp.concatenate(pieces, axis=0)

    b_last = _row_bcast(b, range(chunk - 1, _TILE, chunk), chunk)
    levels = []
    m = chunk
    while m >= 2:
        if m >= 16:
            ref = _row_bcast(b, range(m // 2, _TILE, m), m)
        else:
            ref = refs_in_groups(list(range(m // 2, 8, m)))
        levels.append(jnp.exp(-jnp.abs(b - ref)))
        m //= 2
    return jnp.exp(b), jnp.exp(b_last - b), levels


def _hgrn_out(o, ga, gn):
    return (_rms(o, gn) * jax.nn.sigmoid(ga)).astype(BF)


def _hgrn_p_body(qa_ref, fa_ref, ia_ref, ga_ref, lb_ref, gn_ref, nm_ref, mk_ref,
                 o_ref, s_ref, st_ref, *, nchunks, chunk_len, nh):
    ti = pl.program_id(2)

    @pl.when(ti == 0)
    def _():
        st_ref[...] = jnp.zeros_like(st_ref)

    lb = _hgrn_lb(lb_ref[...])
    gn = gn_ref[...]

    def chunk(c, carry):
        rows = pl.ds(pl.multiple_of(c * _TILE, _TILE), _TILE)
        qa, fa, ia, ga = (r[0, rows, :] for r in (qa_ref, fa_ref, ia_ref, ga_ref))
        sts = [st_ref[j] for j in range(nh)]
        hs = range(nh)
        cols = [slice(j * A_DK, (j + 1) * A_DK) for j in hs]
        q, logf, k = _hgrn_gates(qa, fa, lb)
        eb, er, levels = _hgrn_decays(nm_ref, logf, chunk_len)
        qe = (q * eb).astype(BF)
        ke = (k * er).astype(BF)
        vs = [ia[:, c].astype(BF) for c in cols]
        o_inter = [_dot_nt(qe[:, c], sts[j].astype(BF)) for j, c in enumerate(cols)]
        scs = [_hgrn_scores(q[:, c], k[:, c], [e[:, c] for e in levels], mk_ref) for c in cols]
        os_ = [o_inter[j] + _dot(scs[j].astype(BF), vs[j]) for j in hs]
        upd = [_dot_tn(vs[j], ke[:, c]) for j, c in enumerate(cols)]
        for j, c in enumerate(cols):
            st_ref[j] = sts[j] * eb[_TILE - 1:_TILE, c] + upd[j]
        o_ref[0, rows, :] = jnp.concatenate(
            [_hgrn_out(os_[j], ga[:, c], gn[:, c]) for j, c in enumerate(cols)], axis=1)
        return carry

    lax.fori_loop(0, nchunks, chunk, 0)

    @pl.when(ti == pl.num_programs(2) - 1)
    def _():
        for j in range(nh):
            s_ref[0, j] = st_ref[j].T


def _hgrn_prompt(proj, lb_logits, g_norm, b, t, tt=512, nh=4):
    nm, mk = _hgrn_consts(A_CHUNK)
    h = A_HEADS
    hg = h // nh
    proj3 = proj.reshape(b, t, proj.shape[-1])

    def col(off):
        return pl.BlockSpec((1, tt, nh * A_DK), lambda bi, hi, ti: (bi, ti, off + hi))

    body = functools.partial(_hgrn_p_body, nchunks=tt // _TILE, chunk_len=A_CHUNK, nh=nh)
    return pl.pallas_call(
        body,
        grid=(b, hg, t // tt),
        in_specs=[col(0), col(hg), col(2 * hg), col(3 * hg),
                  pl.BlockSpec((lb_logits.shape[0], nh * A_DK), lambda bi, hi, ti: (0, hi)),
                  pl.BlockSpec((1, nh * A_DV), lambda bi, hi, ti: (0, hi)),
                  pl.BlockSpec(nm.shape, lambda bi, hi, ti: (0, 0)),
                  pl.BlockSpec(mk.shape, lambda bi, hi, ti: (0, 0))],
        out_specs=[pl.BlockSpec((1, tt, nh * A_DV), lambda bi, hi, ti: (bi, ti, hi)),
                   pl.BlockSpec((1, nh, A_DK, A_DV), lambda bi, hi, ti: (bi, hi, 0, 0))],
        out_shape=[jax.ShapeDtypeStruct((b, t, h * A_DV), BF),
                   jax.ShapeDtypeStruct((b, h, A_DK, A_DV), F32)],
        scratch_shapes=[pltpu.VMEM((nh, A_DV, A_DK), F32)],
        compiler_params=_params(("arbitrary", "arbitrary", "arbitrary")),
        name="hgrn_prompt",
    )(proj3, proj3, proj3, proj3, lb_logits, g_norm, nm, mk)


def _hgrn_s_body(qa_ref, fa_ref, ia_ref, ga_ref, lb_ref, gn_ref, nm_ref, mk_ref, sel_ref,
                 s0_ref, o_ref, s_ref, *, chunk_len, gb):
    lb = _hgrn_lb(lb_ref[...])
    q, logf, k = _hgrn_gates(qa_ref[...], fa_ref[...], lb)
    vf = ia_ref[...]
    v = vf.astype(BF)
    eb, er, levels = _hgrn_decays(nm_ref, logf, chunk_len)
    sel = sel_ref[...]
    hi, mid, lo = _split3(logf)
    selb = sel.astype(BF)
    dec = jnp.exp(_dot_tn(hi, selb) + _dot_tn(mid, selb) + _dot_tn(lo, selb))
    qe = q * eb
    q_blk = (jnp.concatenate([qe] * gb, axis=1) * sel).astype(BF)
    v_blk = (jnp.concatenate([vf] * gb, axis=1) * sel).astype(BF)
    s0 = s0_ref[0, :, 0]
    o = _dot(q_blk, s0.reshape(gb * A_DK, A_DV).astype(BF))
    sc = _hgrn_scores(q, k, levels, mk_ref)
    o = o + _dot(sc.astype(BF), v)
    upd = _dot_tn((k * er).astype(BF), v_blk)
    for j in range(gb):
        cols = slice(j * A_DV, (j + 1) * A_DV)
        s_ref[0, j, 0] = dec[:, cols] * s0[j] + upd[:, cols]
    o_ref[...] = _hgrn_out(o, ga_ref[...], gn_ref[...])


def _hgrn_sample(proj, lb_logits, g_norm, state, nb, t):
    gb = _TILE // t
    nm, mk = _hgrn_consts(t)
    h = A_HEADS
    sel = np.zeros((_TILE, gb * A_DV), np.float32)
    for j in range(gb):
        sel[j * t:(j + 1) * t, j * A_DV:(j + 1) * A_DV] = 1.0
    sel = jnp.asarray(sel)

    def col(off):
        return pl.BlockSpec((_TILE, A_DK), lambda i, hi: (i, off + hi))

    body = functools.partial(_hgrn_s_body, chunk_len=t, gb=gb)
    st_spec = pl.BlockSpec((1, gb, 1, A_DK, A_DV), lambda i, hi: (0, i, hi, 0, 0))
    return pl.pallas_call(
        body,
        grid=(nb // gb, h),
        in_specs=[col(0), col(h), col(2 * h), col(3 * h),
                  pl.BlockSpec((lb_logits.shape[0], A_DK), lambda i, hi: (0, hi)),
                  pl.BlockSpec((1, A_DV), lambda i, hi: (0, hi)),
                  pl.BlockSpec(nm.shape, lambda i, hi: (0, 0)),
                  pl.BlockSpec(mk.shape, lambda i, hi: (0, 0)),
                  pl.BlockSpec(sel.shape, lambda i, hi: (0, 0)),
                  st_spec],
        out_specs=[pl.BlockSpec((_TILE, A_DV), lambda i, hi: (i, hi)), st_spec],
        out_shape=[jax.ShapeDtypeStruct((nb * t, h * A_DV), BF),
                   jax.ShapeDtypeStruct(state.shape, F32)],
        compiler_params=_params(("arbitrary", "arbitrary")),
        name="hgrn_sample",
    )(proj, proj, proj, proj, lb_logits, g_norm, nm, mk, sel, state)


def _mla_prep_body(p_ref, cos_ref, sin_ref, gq_ref, gkv_ref, wq_ref, wuk_ref,
                   q_ref, kv_ref, ckv_ref, kpe_ref):
    p = p_ref[...]
    cos = cos_ref[...]
    sin = sin_ref[...]
    qn = _rms(p[:, :Q_RANK], gq_ref[...]).astype(BF)
    qf = _dot(qn, wq_ref[...])
    hw = B_HEADS * NOPE_DIM
    for h in range(B_HEADS):
        c0 = h * NOPE_DIM
        q_lat = _dot(qf[:, c0:c0 + NOPE_DIM].astype(BF), wuk_ref[h])
        q_pe = qf[:, hw + c0:hw + c0 + LANES] * cos + qf[:, 2 * hw + c0:2 * hw + c0 + LANES] * sin
        q_ref[:, h * QK_DIM:h * QK_DIM + KV_RANK] = q_lat.astype(q_ref.dtype)
        q_ref[:, h * QK_DIM + KV_RANK:(h + 1) * QK_DIM] = q_pe.astype(q_ref.dtype)
    ckv = _rms(p[:, Q_RANK:Q_RANK + KV_RANK], gkv_ref[...])
    o = Q_RANK + KV_RANK
    kpe = p[:, o:o + LANES] * cos + p[:, o + LANES:o + 2 * LANES] * sin
    ckv_ref[...] = ckv
    kpe_ref[...] = kpe[:, :ROPE_DIM]
    kv_ref[:, :KV_RANK] = ckv.astype(kv_ref.dtype)
    kv_ref[:, KV_RANK:] = kpe.astype(kv_ref.dtype)


def _mla_prep(proj_mla, cos, sin, g_q, g_kv, wq, wuk, tm, n_pos_tiles, act_dtype):
    rows, pc = proj_mla.shape
    return pl.pallas_call(
        _mla_prep_body,
        grid=(rows // tm,),
        in_specs=[pl.BlockSpec((tm, pc), lambda i: (i, 0)),
                  pl.BlockSpec((tm, LANES), lambda i: (i % n_pos_tiles, 0)),
                  pl.BlockSpec((tm, LANES), lambda i: (i % n_pos_tiles, 0)),
                  pl.BlockSpec((1, Q_RANK), lambda i: (0, 0)),
                  pl.BlockSpec((1, KV_RANK), lambda i: (0, 0)),
                  pl.BlockSpec(wq.shape, lambda i: (0, 0)),
                  pl.BlockSpec(wuk.shape, lambda i: (0, 0, 0))],
        out_specs=[pl.BlockSpec((tm, B_HEADS * QK_DIM), lambda i: (i, 0)),
                   pl.BlockSpec((tm, QK_DIM), lambda i: (i, 0)),
                   pl.BlockSpec((tm, KV_RANK), lambda i: (i, 0)),
                   pl.BlockSpec((tm, ROPE_DIM), lambda i: (i, 0))],
        out_shape=[jax.ShapeDtypeStruct((rows, B_HEADS * QK_DIM), act_dtype),
                   jax.ShapeDtypeStruct((rows, QK_DIM), act_dtype),
                   jax.ShapeDtypeStruct((rows, KV_RANK), F32),
                   jax.ShapeDtypeStruct((rows, ROPE_DIM), F32)],
        compiler_params=_params(("arbitrary",)),
        name="mla_prep",
    )(proj_mla, cos, sin, g_q, g_kv, wq, wuk)


_EXP2_SCALE = MLA_SCALE * math.log2(math.e)


def _lane_blocks(x):
    return [x[:, k * LANES:(k + 1) * LANES] for k in range(x.shape[1] // LANES)]


def _softmax_step(s, v, m_ref, l_ref, acc_ref):
    blocks = _lane_blocks(s)
    m_prev = m_ref[...]
    m_new = jnp.maximum(m_prev, jnp.max(functools.reduce(jnp.maximum, blocks), axis=-1, keepdims=True))
    m_ref[...] = m_new
    alpha = jnp.exp2((m_prev - m_new) * _EXP2_SCALE)
    ps = [jnp.exp2((blk - m_new) * _EXP2_SCALE) for blk in blocks]
    l_ref[...] = alpha * l_ref[...] + functools.reduce(jnp.add, ps)
    pv = _dot(jnp.concatenate(ps, axis=1).astype(BF), v)
    for k, blk in enumerate(_lane_blocks(pv)):
        cols = slice(k * LANES, (k + 1) * LANES)
        acc_ref[:, cols] = alpha * acc_ref[:, cols] + blk


def _softmax_finish(acc, l_part):
    return acc / jnp.sum(l_part, axis=-1, keepdims=True)


def _attn_p_body(q_ref, kv_ref, wuv_ref, o_ref, *scratch, tq, hpar):
    qi = pl.program_id(1)
    row = lax.broadcasted_iota(jnp.int32, (tq, tq), 0)
    col = lax.broadcasted_iota(jnp.int32, (tq, tq), 1)
    stats = [scratch[3 * i:3 * i + 3] for i in range(hpar)]
    for h0 in range(0, B_HEADS, hpar):
        qs = [q_ref[0, :, (h0 + i) * QK_DIM:(h0 + i + 1) * QK_DIM] for i in range(hpar)]
        for acc_ref, m_ref, l_ref in stats:
            m_ref[...] = jnp.full_like(m_ref, -jnp.inf)
            l_ref[...] = jnp.zeros_like(l_ref)
            acc_ref[...] = jnp.zeros_like(acc_ref)

        def block(j, diagonal):
            kv = kv_ref[0, pl.ds(pl.multiple_of(j * tq, tq), tq), :]
            ss = [_dot_nt(qh, kv) for qh in qs]
            for s, (acc_ref, m_ref, l_ref) in zip(ss, stats):
                if diagonal:
                    s = jnp.where(col <= row, s, -jnp.inf)
                _softmax_step(s, kv[:, :KV_RANK], m_ref, l_ref, acc_ref)

        def body(j, carry):
            block(j, False)
            return carry

        lax.fori_loop(0, qi, body, 0)
        block(qi, True)
        for i, (acc_ref, m_ref, l_ref) in enumerate(stats):
            h = h0 + i
            o_lat = _softmax_finish(acc_ref[...], l_ref[...]).astype(BF)
            o_ref[0, :, h * V_DIM:(h + 1) * V_DIM] = _dot(o_lat, wuv_ref[h]).astype(o_ref.dtype)


def _attn_prompt(q_cat, kv_cat, wuv, b, t, tq=512, hpar=2):
    q3 = q_cat.reshape(b, t, q_cat.shape[-1])
    kv3 = kv_cat.reshape(b, t, kv_cat.shape[-1])
    body = functools.partial(_attn_p_body, tq=tq, hpar=hpar)
    return pl.pallas_call(
        body,
        grid=(b, t // tq),
        in_specs=[pl.BlockSpec((1, tq, q3.shape[-1]), lambda bi, qi: (bi, qi, 0)),
                  pl.BlockSpec((1, t, QK_DIM), lambda bi, qi: (bi, 0, 0)),
                  pl.BlockSpec(wuv.shape, lambda bi, qi: (0, 0, 0))],
        out_specs=pl.BlockSpec((1, tq, B_HEADS * V_DIM), lambda bi, qi: (bi, qi, 0)),
        out_shape=jax.ShapeDtypeStruct((b, t, B_HEADS * V_DIM), BF),
        scratch_shapes=[pltpu.VMEM((tq, KV_RANK), F32),
                        pltpu.VMEM((tq, LANES), F32),
                        pltpu.VMEM((tq, LANES), F32)] * hpar,
        compiler_params=_params(("arbitrary", "arbitrary")),
        name="attn_prompt",
    )(q3, kv3, wuv)


def _attn_s_body(pt_ref, q_ref, kvn_ref, ckv_hbm, kpe_hbm, wuv_ref, o_ref,
                 ckv_buf, kpe_buf, sem, qs_ref, acc_ref, m_ref, l_ref,
                 *, nch, ppc, n_pages, t_new, nsplit):
    b = pl.program_id(0)
    keys = ppc * PAGE_SIZE

    def page_copies(bi, c, p):
        slot = c % 2
        page = pt_ref[bi * n_pages + c * ppc + p]
        span = pl.ds(p * PAGE_SIZE, PAGE_SIZE)
        return (pltpu.make_async_copy(ckv_hbm.at[page], ckv_buf.at[slot, span, :], sem.at[slot, 0]),
                pltpu.make_async_copy(kpe_hbm.at[page], kpe_buf.at[slot, :, span], sem.at[slot, 1]))

    def start_all(bi, c):
        for p in range(ppc):
            for cp in page_copies(bi, c, p):
                cp.start()

    def wait_all(bi, c):
        for p in range(ppc):
            for cp in page_copies(bi, c, p):
                cp.wait()

    @pl.when(b == 0)
    def _():
        start_all(b, 0)

    for h in range(B_HEADS):
        qs_ref[h * t_new:(h + 1) * t_new, :] = q_ref[0, :, h * QK_DIM:(h + 1) * QK_DIM]
    m_ref[...] = jnp.full_like(m_ref, -jnp.inf)
    l_ref[...] = jnp.zeros_like(l_ref)
    acc_ref[...] = jnp.zeros_like(acc_ref)
    qs = qs_ref[...]
    qb = qs.astype(BF)
    q_lat = qb[:, :KV_RANK]
    q_pe = qb[:, KV_RANK:KV_RANK + ROPE_DIM]

    for c in range(nch):
        if c + 1 < nch:
            start_all(b, c + 1)
        else:
            @pl.when(b + 1 < pl.num_programs(0))
            def _():
                start_all(b + 1, 0)
        wait_all(b, c)
        slot = c % 2
        part = keys // nsplit
        kcs, ss = [], []
        for i in range(nsplit):
            span = slice(i * part, (i + 1) * part)
            kc = ckv_buf[slot, span, :].astype(BF)
            kpt = kpe_buf[slot, :, span].astype(BF)
            kcs.append(kc)
            ss.append(_dot_nt(q_lat, kc) + _dot(q_pe, kpt))
        for kc, s in zip(kcs, ss):
            _softmax_step(s, kc, m_ref, l_ref, acc_ref)

    kn = kvn_ref[0]
    rows = B_HEADS * t_new
    sn = _dot_nt(qs, kn)
    tq_ = lax.broadcasted_iota(jnp.int32, (rows, t_new), 0) % t_new
    tk_ = lax.broadcasted_iota(jnp.int32, (rows, t_new), 1)
    sn = jnp.where(tk_ <= tq_, sn, -jnp.inf)
    m_prev = m_ref[:, 0:1]
    m_new = jnp.maximum(m_prev, jnp.max(sn, axis=-1, keepdims=True))
    alpha = jnp.exp2((m_prev - m_new) * _EXP2_SCALE)
    pn = jnp.exp2((sn - m_new) * _EXP2_SCALE)
    l = alpha * jnp.sum(l_ref[...], axis=-1, keepdims=True) + jnp.sum(pn, axis=-1, keepdims=True)
    acc = alpha * acc_ref[...]
    vn = kn[:, :KV_RANK].astype(BF).astype(F32)
    pn = pn.astype(BF).astype(F32)
    for j in range(t_new):
        acc = acc + pn[:, j:j + 1] * vn[j:j + 1, :]
    o_lat = (acc / l).astype(BF)
    r = _dot(o_lat, wuv_ref[...])
    for h in range(B_HEADS):
        o_ref[0, :, h * V_DIM:(h + 1) * V_DIM] = r[h * t_new:(h + 1) * t_new, h * V_DIM:(h + 1) * V_DIM]


def _attn_sample(page_table, q_cat, kv_cat, cache_ckv, cache_kpe, wuv_all, nb, t_new, ppc=16, nsplit=2):
    n_pages = page_table.shape[1]
    nch = n_pages // ppc
    assert nch % 2 == 0
    keys = ppc * PAGE_SIZE
    q3 = q_cat.reshape(nb, t_new, q_cat.shape[-1])
    kv3 = kv_cat.reshape(nb, t_new, kv_cat.shape[-1])
    rows = B_HEADS * t_new
    body = functools.partial(_attn_s_body, nch=nch, ppc=ppc, n_pages=n_pages, t_new=t_new, nsplit=nsplit)
    grid_spec = pltpu.PrefetchScalarGridSpec(
        num_scalar_prefetch=1,
        grid=(nb,),
        in_specs=[pl.BlockSpec((1, t_new, q3.shape[-1]), lambda bi, pt: (bi, 0, 0)),
                  pl.BlockSpec((1, t_new, QK_DIM), lambda bi, pt: (bi, 0, 0)),
                  pl.BlockSpec(memory_space=pl.ANY),
                  pl.BlockSpec(memory_space=pl.ANY),
                  pl.BlockSpec(wuv_all.shape, lambda bi, pt: (0, 0))],
        out_specs=pl.BlockSpec((1, t_new, B_HEADS * V_DIM), lambda bi, pt: (bi, 0, 0)),
        scratch_shapes=[pltpu.VMEM((2, keys, KV_RANK), F32),
                        pltpu.VMEM((2, ROPE_DIM, keys), F32),
                        pltpu.SemaphoreType.DMA((2, 2)),
                        pltpu.VMEM((rows, QK_DIM), F32),
                        pltpu.VMEM((rows, KV_RANK), F32),
                        pltpu.VMEM((rows, LANES), F32),
                        pltpu.VMEM((rows, LANES), F32)])
    return pl.pallas_call(
        body,
        grid_spec=grid_spec,
        out_shape=jax.ShapeDtypeStruct((nb, t_new, B_HEADS * V_DIM), F32),
        compiler_params=_params(("arbitrary",)),
        name="attn_sample",
    )(page_table.reshape(-1), q3, kv3, cache_ckv, cache_kpe, wuv_all)


def _mix_body(oa_ref, ob_ref, ga_ref, gb_ref, x_ref, gt_ref, g_ref, wa_ref, wb_ref, wo_ref, o_ref):
    ya = _dot(oa_ref[...].astype(BF), wa_ref[...])
    yb = _dot(ob_ref[...].astype(BF), wb_ref[...])
    merged = jax.nn.sigmoid(ga_ref[...]) * ya + jax.nn.sigmoid(gb_ref[...]) * yb
    z = _dot(merged.astype(BF), wo_ref[...])
    x = x_ref[...]
    o_ref[...] = x + gt_ref[...] * _rms(z, g_ref[...]).reshape(x.shape)


def _mix(o_a, o_b, proj, x, mod, g, wa, wb, wo, gb, tr):
    bm, r, d = x.shape
    tm = gb * tr
    nr = r // tr
    gate0 = 4 * A_HEADS * A_DK // d

    def rows2(width, cb):
        return pl.BlockSpec((tm, width), lambda a, i: (a * nr + i, cb))

    def const(w):
        return pl.BlockSpec(w.shape, lambda a, i: (0, 0))

    return pl.pallas_call(
        _mix_body,
        grid=(bm // gb, nr),
        in_specs=[rows2(o_a.shape[1], 0), rows2(o_b.shape[1], 0),
                  rows2(d, gate0), rows2(d, gate0 + 1),
                  pl.BlockSpec((gb, tr, d), lambda a, i: (a, i, 0)),
                  pl.BlockSpec((gb, 1, d), lambda a, i: (a, 0, 2)),
                  const(g), const(wa), const(wb), const(wo)],
        out_specs=pl.BlockSpec((gb, tr, d), lambda a, i: (a, i, 0)),
        out_shape=jax.ShapeDtypeStruct(x.shape, F32),
        compiler_params=_params(("arbitrary", "arbitrary")),
        name="mix",
    )(o_a, o_b, proj, proj, x, mod, g, wa, wb, wo)


def _mlp_body(x_ref, sh_ref, sc_ref, gt_ref, g1_ref, g2_ref, wu_ref, wd_ref, o_ref, h_ref, acc_ref):
    f = pl.program_id(2)

    @pl.when(f == 0)
    def _():
        x = x_ref[...]
        h = _rms(x, g1_ref[...]) * (1.0 + sc_ref[...]) + sh_ref[...]
        h_ref[...] = h.reshape(h_ref.shape).astype(BF)
        acc_ref[...] = jnp.zeros_like(acc_ref)

    u = jnp.maximum(_dot(h_ref[...], wu_ref[...]), 0.0)
    acc_ref[...] += _dot((u * u).astype(BF), wd_ref[...])

    @pl.when(f == pl.num_programs(2) - 1)
    def _():
        x = x_ref[...]
        o_ref[...] = x + gt_ref[...] * _rms(acc_ref[...], g2_ref[...]).reshape(x.shape)


def _mlp(x, mod, g1, g2, wu, wd, gb, tr, tf=1024):
    bm, r, d = x.shape
    tm = gb * tr
    nr = r // tr
    ff = wu.shape[1]

    def modspec(piece):
        return pl.BlockSpec((gb, 1, d), lambda a, i, f: (a, 0, piece))

    return pl.pallas_call(
        _mlp_body,
        grid=(bm // gb, nr, ff // tf),
        in_specs=[pl.BlockSpec((gb, tr, d), lambda a, i, f: (a, i, 0)),
                  modspec(3), modspec(4), modspec(5),
                  pl.BlockSpec((1, d), lambda a, i, f: (0, 0)),
                  pl.BlockSpec((1, d), lambda a, i, f: (0, 0)),
                  pl.BlockSpec((d, tf), lambda a, i, f: (0, f)),
                  pl.BlockSpec((tf, d), lambda a, i, f: (f, 0))],
        out_specs=pl.BlockSpec((gb, tr, d), lambda a, i, f: (a, i, 0)),
        out_shape=jax.ShapeDtypeStruct(x.shape, F32),
        scratch_shapes=[pltpu.VMEM((tm, d), BF), pltpu.VMEM((tm, d), F32)],
        compiler_params=_params(("arbitrary", "arbitrary", "arbitrary")),
        name="mlp",
    )(x, mod, mod, mod, g1, g2, wu, wd)


def _rope_tables(pos):
    half = ROPE_DIM // 2
    inv = ROPE_THETA ** (-jnp.arange(half, dtype=F32) / half)
    ang = pos.astype(F32)[:, None] * inv[None, :]
    z = jnp.zeros((pos.shape[0], LANES - ROPE_DIM), F32)
    cos = jnp.cos(ang)
    sin = jnp.sin(ang)
    return jnp.concatenate([cos, cos, z], axis=1), jnp.concatenate([sin, sin, z], axis=1)


def _swap_halves(w):
    half = w.shape[-1] // 2
    return jnp.concatenate([-w[..., half:], w[..., :half]], axis=-1)


def _pad_lanes(w):
    pad = [(0, 0)] * (w.ndim - 1) + [(0, LANES - w.shape[-1])]
    return jnp.pad(w, pad)


def kernel(x_prompt, x_sample, c_prompt, c_sample, cache_ckv, cache_kpe, state_hgrn, page_table, w_ada, b_ada, g_pre_mix, g_post_mix, g_pre_mlp, g_post_mlp, w_in, lb_logits, g_hgrn_norm, w_a_out, g_q_norm, w_q_up, g_kv_norm, w_kv_up, w_b_out, w_o, w_up, w_down):
    depth = w_in.shape[0]
    assert depth == 1
    batch, seq, d = x_prompt.shape
    nb, t_new, _ = x_sample.shape
    past_len = page_table.shape[1] * PAGE_SIZE
    hk = A_HEADS * A_DK

    wit = w_in[0].T
    o_qd = 3 * hk + A_HEADS * A_DV
    o_kpe = o_qd + Q_RANK + KV_RANK
    o_gate = o_kpe + ROPE_DIM
    w_main = jnp.concatenate([wit[:o_qd], wit[o_gate:]], axis=0).astype(BF)
    w_kpe = wit[o_kpe:o_gate]
    half = ROPE_DIM // 2
    w_kpe_swapped = jnp.concatenate([-w_kpe[half:], w_kpe[:half]], axis=0)
    zpad = jnp.zeros((LANES - ROPE_DIM, d), F32)
    w_mla = jnp.concatenate([wit[o_qd:o_kpe], w_kpe, zpad, w_kpe_swapped, zpad], axis=0).astype(BF)
    wq = w_q_up[0].reshape(Q_RANK, B_HEADS, NOPE_DIM + ROPE_DIM)
    wq_pe = wq[..., NOPE_DIM:]
    wq_cat = jnp.concatenate([wq[..., :NOPE_DIM].reshape(Q_RANK, -1),
                              _pad_lanes(wq_pe).reshape(Q_RANK, -1),
                              _pad_lanes(_swap_halves(wq_pe)).reshape(Q_RANK, -1)], axis=1).astype(BF)
    wkv = w_kv_up[0].reshape(KV_RANK, B_HEADS, NOPE_DIM + V_DIM)
    wuk = wkv[..., :NOPE_DIM].transpose(1, 2, 0).astype(BF)
    wuv = wkv[..., NOPE_DIM:].transpose(1, 0, 2).astype(BF)
    wuv_all = wkv[..., NOPE_DIM:].reshape(KV_RANK, B_HEADS * V_DIM).astype(BF)
    wa = w_a_out[0].astype(BF)
    wb = w_b_out[0].astype(BF)
    wo = w_o[0].astype(BF)
    wu = w_up[0].astype(BF)
    wd = w_down[0].astype(BF)

    c_all = jnp.concatenate([c_prompt, c_sample], axis=0)
    mod = _ada(c_all, w_ada[0], b_ada[0][None, :])
    mod = mod.reshape(batch + nb, 1, 6 * d)
    mod_p, mod_s = mod[:batch], mod[batch:]

    cos_p, sin_p = _rope_tables(jnp.arange(seq))
    cos_s, sin_s = _rope_tables(past_len + jnp.arange(t_new))
    gs = _TILE
    reps = gs
    cos_s = jnp.tile(cos_s, (reps, 1))
    sin_s = jnp.tile(sin_s, (reps, 1))

    def layer(x, mod_g, gb, tr, hgrn, attend, cos, sin, n_pos_tiles, act_dtype):
        tm = gb * tr
        proj = _proj(x, mod_g, g_pre_mix, w_main, gb, tr, 2048)
        proj_mla = _proj(x, mod_g, g_pre_mix, w_mla, gb, tr, w_mla.shape[0])
        o_a, s_fin = hgrn(proj)
        q_cat, kv_cat, ckv, kpe = _mla_prep(proj_mla, cos, sin, g_q_norm, g_kv_norm, wq_cat, wuk,
                                            tm, n_pos_tiles, act_dtype)
        o_b = attend(q_cat, kv_cat)
        mix_gb, mix_tr = (1, tr // 2) if gb == 1 else (gb // 2, tr)
        x1 = _mix(o_a.reshape(-1, o_a.shape[-1]), o_b.reshape(-1, o_b.shape[-1]), proj, x, mod_g,
                  g_post_mix, wa, wb, wo, mix_gb, mix_tr)
        y = _mlp(x1, mod_g, g_pre_mlp, g_post_mlp, wu, wd, gb, tr)
        return y, ckv, kpe, s_fin

    kpe_pages_t = jnp.swapaxes(cache_kpe[0], 1, 2)

    tr_p = 512
    y_p, ckv_p, kpe_p, s_p = layer(
        x_prompt, mod_p, 1, tr_p,
        lambda proj: _hgrn_prompt(proj, lb_logits, g_hgrn_norm, batch, seq),
        lambda q, kv: _attn_prompt(q, kv, wuv, batch, seq),
        cos_p, sin_p, seq // tr_p, BF)
    y_s, ckv_s, kpe_s, s_s = layer(
        x_sample, mod_s, gs, t_new,
        lambda proj: _hgrn_sample(proj, lb_logits, g_hgrn_norm, state_hgrn, nb, t_new),
        lambda q, kv: _attn_sample(page_table, q, kv, cache_ckv[0], kpe_pages_t, wuv_all, nb, t_new),
        cos_s, sin_s, 1, F32)

    return (y_p, y_s,
            ckv_p.reshape(1, batch, seq, KV_RANK), kpe_p.reshape(1, batch, seq, ROPE_DIM),
            s_p[None],
            ckv_s.reshape(1, nb, t_new, KV_RANK), kpe_s.reshape(1, nb, t_new, ROPE_DIM),
            s_s)
```

```python
import functools
import math

import numpy as np
import jax
import jax.numpy as jnp
from jax import lax
from jax.experimental import pallas as pl
from jax.experimental.pallas import tpu as pltpu

BF = jnp.bfloat16
F32 = jnp.float32

D_MODEL = 2048
A_HEADS = 8
A_DK = 128
A_DV = 128
A_CHUNK = 64
B_HEADS = 8
Q_RANK = 512
KV_RANK = 512
NOPE_DIM = 128
ROPE_DIM = 64
V_DIM = 128
ROPE_THETA = 10000.0
MLA_SCALE = (NOPE_DIM + ROPE_DIM) ** -0.5
D_FF = 4 * D_MODEL
EPS = 1e-6
PAGE_SIZE = 128

LANES = 128
QK_DIM = KV_RANK + LANES
VMEM_LIMIT = 56 * 1024 * 1024

_NT = (((1,), (1,)), ((), ()))
_TN = (((0,), (0,)), ((), ()))


def _dot(a, b):
    return jnp.dot(a, b, preferred_element_type=F32)


def _dot_nt(a, b):
    return lax.dot_general(a, b, _NT, preferred_element_type=F32)


def _dot_tn(a, b):
    return lax.dot_general(a, b, _TN, preferred_element_type=F32)


def _rms(x, g):
    r = lax.rsqrt(jnp.mean(x * x, axis=-1, keepdims=True) + EPS)
    return (x * r) * g


def _params(sem):
    return pltpu.CompilerParams(dimension_semantics=sem, vmem_limit_bytes=VMEM_LIMIT)


def _ada_body(c_ref, w_ref, b_ref, o_ref):
    c = c_ref[...]
    s = (c * jax.nn.sigmoid(c)).astype(BF)
    o_ref[...] = _dot(s, w_ref[...].astype(BF)) + b_ref[...]


def _ada(c_all, w_ada, b_ada, tn=1024):
    m, d = c_all.shape
    n = w_ada.shape[1]
    return pl.pallas_call(
        _ada_body,
        grid=(n // tn,),
        in_specs=[pl.BlockSpec((m, d), lambda j: (0, 0)),
                  pl.BlockSpec((d, tn), lambda j: (0, j)),
                  pl.BlockSpec((1, tn), lambda j: (0, j))],
        out_specs=pl.BlockSpec((m, tn), lambda j: (0, j)),
        out_shape=jax.ShapeDtypeStruct((m, n), F32),
        compiler_params=_params(("arbitrary",)),
        name="ada",
    )(c_all, w_ada, b_ada)


def _proj_body(x_ref, sh_ref, sc_ref, g_ref, w_ref, o_ref, h_ref):
    @pl.when(pl.program_id(2) == 0)
    def _():
        x = x_ref[...]
        h = _rms(x, g_ref[...]) * (1.0 + sc_ref[...]) + sh_ref[...]
        h_ref[...] = h.reshape(h_ref.shape).astype(BF)

    o_ref[...] = _dot_nt(h_ref[...], w_ref[...])


def _proj(x, mod, g, w_t, gb, tr, tn):
    bm, r, d = x.shape
    n = w_t.shape[0]
    tm = gb * tr
    nr = r // tr
    return pl.pallas_call(
        _proj_body,
        grid=(bm // gb, nr, n // tn),
        in_specs=[pl.BlockSpec((gb, tr, d), lambda a, i, j: (a, i, 0)),
                  pl.BlockSpec((gb, 1, d), lambda a, i, j: (a, 0, 0)),
                  pl.BlockSpec((gb, 1, d), lambda a, i, j: (a, 0, 1)),
                  pl.BlockSpec((1, d), lambda a, i, j: (0, 0)),
                  pl.BlockSpec((tn, d), lambda a, i, j: (j, 0))],
        out_specs=pl.BlockSpec((tm, tn), lambda a, i, j: (a * nr + i, j)),
        out_shape=jax.ShapeDtypeStruct((bm * r, n), F32),
        scratch_shapes=[pltpu.VMEM((tm, d), BF)],
        compiler_params=_params(("arbitrary", "arbitrary", "arbitrary")),
        name="proj",
    )(x, mod, mod, g, w_t)


_TILE = 64


def _hgrn_consts(chunk):
    t = np.arange(_TILE)[:, None]
    u = np.arange(_TILE)[None, :]
    tril = ((t // chunk) == (u // chunk)) & (u <= t)
    masks = []
    m = chunk
    while m >= 2:
        h = m // 2
        masks.append(((t // m) == (u // m)) & ((t % m) >= h) & ((u % m) < h))
        m = h
    masks.append(t == u)
    mk = np.concatenate([a.astype(np.float32) for a in masks], axis=0)
    return jnp.asarray(tril.astype(np.float32), BF), jnp.asarray(mk, F32)


def _split3(x):
    hi = x.astype(BF)
    r1 = x - hi.astype(F32)
    mid = r1.astype(BF)
    lo = (r1 - mid.astype(F32)).astype(BF)
    return hi, mid, lo


def _hgrn_gates(qa, fa, lb):
    q = qa * jax.nn.sigmoid(qa)
    f = lb + (1.0 - lb) * jax.nn.sigmoid(fa)
    return q, jnp.log(f), 1.0 - f


def _hgrn_lb(lbl):
    e = jnp.exp(lbl - jnp.max(lbl, axis=0, keepdims=True))
    return e[0:1] / jnp.sum(e, axis=0, keepdims=True)


def _hgrn_scores(q, k, levels, mk_ref):
    nlev = len(levels)
    sc = jnp.where(mk_ref[nlev * _TILE:(nlev + 1) * _TILE, :] > 0.5,
                   _dot_nt(q.astype(BF), k.astype(BF)), 0.0)
    for l, e in enumerate(levels):
        p = _dot_nt((q * e).astype(BF), (k * e).astype(BF))
        sc = sc + jnp.where(mk_ref[l * _TILE:(l + 1) * _TILE, :] > 0.5, p, 0.0)
    return sc


def _row_bcast(x, rows, n):
    w = x.shape[1]
    return jnp.concatenate([jnp.broadcast_to(x[r:r + 1, :], (n, w)) for r in rows], axis=0)


def _hgrn_decays(tril_ref, logf, chunk):
    w = logf.shape[1]
    a = _dot(tril_ref[...], jnp.concatenate(_split3(logf), axis=1))
    b = (a[:, :w] + a[:, w:2 * w] + a[:, 2 * w:]) * math.log2(math.e)
    sub = lax.broadcasted_iota(jnp.int32, (8, w), 0)

    def refs_in_groups(offs):
        span = 8 // len(offs)
        pieces = []
        for g in range(_TILE // 8):
            rows = [jnp.broadcast_to(b[g * 8 + o:g * 8 + o + 1, :], (8, w)) for o in offs]
            ref = rows[-1]
            for i in range(len(offs) - 2, -1, -1):
                ref = jnp.where(sub < (i + 1) * span, rows[i], ref)
            pieces.append(ref)
        return jnp.concatenate(pieces, axis=0)

    b_last = _row_bcast(b, range(chunk - 1, _TILE, chunk), chunk)
    levels = []
    m = chunk
    while m >= 2:
        if m >= 16:
            ref = _row_bcast(b, range(m // 2, _TILE, m), m)
        else:
            ref = refs_in_groups(list(range(m // 2, 8, m)))
        levels.append(jnp.exp2(-jnp.abs(b - ref)))
        m //= 2
    return jnp.exp2(b), jnp.exp2(b_last - b), levels


def _hgrn_out(o, ga, gn):
    return (_rms(o, gn) * jax.nn.sigmoid(ga)).astype(BF)


def _hgrn_p_body(qa_ref, fa_ref, ia_ref, ga_ref, lb_ref, gn_ref, nm_ref, mk_ref,
                 o_ref, s_ref, st_ref, *, nchunks, chunk_len, nh):
    ti = pl.program_id(2)

    @pl.when(ti == 0)
    def _():
        st_ref[...] = jnp.zeros_like(st_ref)

    lb = _hgrn_lb(lb_ref[...])
    gn = gn_ref[...]

    def chunk(c, carry):
        rows = pl.ds(pl.multiple_of(c * _TILE, _TILE), _TILE)
        qa, fa, ia, ga = (r[0, rows, :] for r in (qa_ref, fa_ref, ia_ref, ga_ref))
        sts = [st_ref[j] for j in range(nh)]
        hs = range(nh)
        cols = [slice(j * A_DK, (j + 1) * A_DK) for j in hs]
        q, logf, k = _hgrn_gates(qa, fa, lb)
        eb, er, levels = _hgrn_decays(nm_ref, logf, chunk_len)
        qe = (q * eb).astype(BF)
        ke = (k * er).astype(BF)
        vs = [ia[:, c].astype(BF) for c in cols]
        o_inter = [_dot_nt(qe[:, c], sts[j].astype(BF)) for j, c in enumerate(cols)]
        scs = [_hgrn_scores(q[:, c], k[:, c], [e[:, c] for e in levels], mk_ref) for c in cols]
        os_ = [o_inter[j] + _dot(scs[j].astype(BF), vs[j]) for j in hs]
        upd = [_dot_tn(vs[j], ke[:, c]) for j, c in enumerate(cols)]
        for j, c in enumerate(cols):
            st_ref[j] = sts[j] * eb[_TILE - 1:_TILE, c] + upd[j]
        o_ref[0, rows, :] = jnp.concatenate(
            [_hgrn_out(os_[j], ga[:, c], gn[:, c]) for j, c in enumerate(cols)], axis=1)
        return carry

    lax.fori_loop(0, nchunks, chunk, 0)

    @pl.when(ti == pl.num_programs(2) - 1)
    def _():
        for j in range(nh):
            s_ref[0, j] = st_ref[j].T


def _hgrn_prompt(proj, lb_logits, g_norm, b, t, tt=512, nh=4):
    nm, mk = _hgrn_consts(A_CHUNK)
    h = A_HEADS
    hg = h // nh
    proj3 = proj.reshape(b, t, proj.shape[-1])

    def col(off):
        return pl.BlockSpec((1, tt, nh * A_DK), lambda bi, hi, ti: (bi, ti, off + hi))

    body = functools.partial(_hgrn_p_body, nchunks=tt // _TILE, chunk_len=A_CHUNK, nh=nh)
    return pl.pallas_call(
        body,
        grid=(b, hg, t // tt),
        in_specs=[col(0), col(hg), col(2 * hg), col(3 * hg),
                  pl.BlockSpec((lb_logits.shape[0], nh * A_DK), lambda bi, hi, ti: (0, hi)),
                  pl.BlockSpec((1, nh * A_DV), lambda bi, hi, ti: (0, hi)),
                  pl.BlockSpec(nm.shape, lambda bi, hi, ti: (0, 0)),
                  pl.BlockSpec(mk.shape, lambda bi, hi, ti: (0, 0))],
        out_specs=[pl.BlockSpec((1, tt, nh * A_DV), lambda bi, hi, ti: (bi, ti, hi)),
                   pl.BlockSpec((1, nh, A_DK, A_DV), lambda bi, hi, ti: (bi, hi, 0, 0))],
        out_shape=[jax.ShapeDtypeStruct((b, t, h * A_DV), BF),
                   jax.ShapeDtypeStruct((b, h, A_DK, A_DV), F32)],
        scratch_shapes=[pltpu.VMEM((nh, A_DV, A_DK), F32)],
        compiler_params=_params(("arbitrary", "arbitrary", "arbitrary")),
        name="hgrn_prompt",
    )(proj3, proj3, proj3, proj3, lb_logits, g_norm, nm, mk)


def _hgrn_s_body(qa_ref, fa_ref, ia_ref, ga_ref, lb_ref, gn_ref, nm_ref, mk_ref, sel_ref,
                 s0_ref, o_ref, s_ref, *, chunk_len, gb):
    lb = _hgrn_lb(lb_ref[...])
    q, logf, k = _hgrn_gates(qa_ref[...], fa_ref[...], lb)
    vf = ia_ref[...]
    v = vf.astype(BF)
    eb, er, levels = _hgrn_decays(nm_ref, logf, chunk_len)
    sel = sel_ref[...]
    parts = _split3(logf)
    selb = sel.astype(BF)
    dec = jnp.exp(_dot_tn(jnp.concatenate(parts, axis=0), jnp.concatenate([selb] * len(parts), axis=0)))
    qe = q * eb
    q_blk = (jnp.concatenate([qe] * gb, axis=1) * sel).astype(BF)
    v_blk = (jnp.concatenate([vf] * gb, axis=1) * sel).astype(BF)
    s0 = s0_ref[0, :, 0]
    o = _dot(q_blk, s0.reshape(gb * A_DK, A_DV).astype(BF))
    sc = _hgrn_scores(q, k, levels, mk_ref)
    o = o + _dot(sc.astype(BF), v)
    upd = _dot_tn((k * er).astype(BF), v_blk)
    for j in range(gb):
        cols = slice(j * A_DV, (j + 1) * A_DV)
        s_ref[0, j, 0] = dec[:, cols] * s0[j] + upd[:, cols]
    o_ref[...] = _hgrn_out(o, ga_ref[...], gn_ref[...])


def _hgrn_sample(proj, lb_logits, g_norm, state, nb, t):
    gb = _TILE // t
    nm, mk = _hgrn_consts(t)
    h = A_HEADS
    sel = np.zeros((_TILE, gb * A_DV), np.float32)
    for j in range(gb):
        sel[j * t:(j + 1) * t, j * A_DV:(j + 1) * A_DV] = 1.0
    sel = jnp.asarray(sel)

    def col(off):
        return pl.BlockSpec((_TILE, A_DK), lambda i, hi: (i, off + hi))

    body = functools.partial(_hgrn_s_body, chunk_len=t, gb=gb)
    st_spec = pl.BlockSpec((1, gb, 1, A_DK, A_DV), lambda i, hi: (0, i, hi, 0, 0))
    return pl.pallas_call(
        body,
        grid=(nb // gb, h),
        in_specs=[col(0), col(h), col(2 * h), col(3 * h),
                  pl.BlockSpec((lb_logits.shape[0], A_DK), lambda i, hi: (0, hi)),
                  pl.BlockSpec((1, A_DV), lambda i, hi: (0, hi)),
                  pl.BlockSpec(nm.shape, lambda i, hi: (0, 0)),
                  pl.BlockSpec(mk.shape, lambda i, hi: (0, 0)),
                  pl.BlockSpec(sel.shape, lambda i, hi: (0, 0)),
                  st_spec],
        out_specs=[pl.BlockSpec((_TILE, A_DV), lambda i, hi: (i, hi)), st_spec],
        out_shape=[jax.ShapeDtypeStruct((nb * t, h * A_DV), BF),
                   jax.ShapeDtypeStruct(state.shape, F32)],
        compiler_params=_params(("arbitrary", "arbitrary")),
        name="hgrn_sample",
    )(proj, proj, proj, proj, lb_logits, g_norm, nm, mk, sel, state)


def _mla_prep_body(p_ref, cos_ref, sin_ref, gq_ref, gkv_ref, wq_ref, wuk_ref,
                   q_ref, kv_ref, ckv_ref, kpe_ref):
    p = p_ref[...]
    cos = cos_ref[...]
    sin = sin_ref[...]
    qn = _rms(p[:, :Q_RANK], gq_ref[...]).astype(BF)
    qf = _dot(qn, wq_ref[...])
    hw = B_HEADS * NOPE_DIM
    for h in range(B_HEADS):
        c0 = h * NOPE_DIM
        q_lat = _dot(qf[:, c0:c0 + NOPE_DIM].astype(BF), wuk_ref[h])
        q_pe = qf[:, hw + c0:hw + c0 + LANES] * cos + qf[:, 2 * hw + c0:2 * hw + c0 + LANES] * sin
        q_ref[:, h * QK_DIM:h * QK_DIM + KV_RANK] = q_lat.astype(q_ref.dtype)
        q_ref[:, h * QK_DIM + KV_RANK:(h + 1) * QK_DIM] = q_pe.astype(q_ref.dtype)
    ckv = _rms(p[:, Q_RANK:Q_RANK + KV_RANK], gkv_ref[...])
    o = Q_RANK + KV_RANK
    kpe = p[:, o:o + LANES] * cos + p[:, o + LANES:o + 2 * LANES] * sin
    ckv_ref[...] = ckv
    kpe_ref[...] = kpe[:, :ROPE_DIM]
    kv_ref[:, :KV_RANK] = ckv.astype(kv_ref.dtype)
    kv_ref[:, KV_RANK:] = kpe.astype(kv_ref.dtype)


def _mla_prep(proj_mla, cos, sin, g_q, g_kv, wq, wuk, tm, n_pos_tiles, act_dtype):
    rows, pc = proj_mla.shape
    return pl.pallas_call(
        _mla_prep_body,
        grid=(rows // tm,),
        in_specs=[pl.BlockSpec((tm, pc), lambda i: (i, 0)),
                  pl.BlockSpec((tm, LANES), lambda i: (i % n_pos_tiles, 0)),
                  pl.BlockSpec((tm, LANES), lambda i: (i % n_pos_tiles, 0)),
                  pl.BlockSpec((1, Q_RANK), lambda i: (0, 0)),
                  pl.BlockSpec((1, KV_RANK), lambda i: (0, 0)),
                  pl.BlockSpec(wq.shape, lambda i: (0, 0)),
                  pl.BlockSpec(wuk.shape, lambda i: (0, 0, 0))],
        out_specs=[pl.BlockSpec((tm, B_HEADS * QK_DIM), lambda i: (i, 0)),
                   pl.BlockSpec((tm, QK_DIM), lambda i: (i, 0)),
                   pl.BlockSpec((tm, KV_RANK), lambda i: (i, 0)),
                   pl.BlockSpec((tm, ROPE_DIM), lambda i: (i, 0))],
        out_shape=[jax.ShapeDtypeStruct((rows, B_HEADS * QK_DIM), act_dtype),
                   jax.ShapeDtypeStruct((rows, QK_DIM), act_dtype),
                   jax.ShapeDtypeStruct((rows, KV_RANK), F32),
                   jax.ShapeDtypeStruct((rows, ROPE_DIM), F32)],
        compiler_params=_params(("arbitrary",)),
        name="mla_prep",
    )(proj_mla, cos, sin, g_q, g_kv, wq, wuk)


_EXP2_SCALE = MLA_SCALE * math.log2(math.e)


def _lane_blocks(x):
    return [x[:, k * LANES:(k + 1) * LANES] for k in range(x.shape[1] // LANES)]


def _softmax_step(s, v, m_ref, l_ref, acc_ref):
    blocks = _lane_blocks(s)
    m_prev = m_ref[...]
    m_new = jnp.maximum(m_prev, jnp.max(functools.reduce(jnp.maximum, blocks), axis=-1, keepdims=True))
    m_ref[...] = m_new
    alpha = jnp.exp2((m_prev - m_new) * _EXP2_SCALE)
    ps = [jnp.exp2((blk - m_new) * _EXP2_SCALE) for blk in blocks]
    l_ref[...] = alpha * l_ref[...] + functools.reduce(jnp.add, ps)
    pv = _dot(jnp.concatenate(ps, axis=1).astype(BF), v)
    for k, blk in enumerate(_lane_blocks(pv)):
        cols = slice(k * LANES, (k + 1) * LANES)
        acc_ref[:, cols] = alpha * acc_ref[:, cols] + blk


def _softmax_finish(acc, l_part):
    return acc / jnp.sum(l_part, axis=-1, keepdims=True)


def _attn_p_body(q_ref, kv_ref, wuv_ref, o_ref, *scratch, tq, hpar):
    qi = pl.program_id(1)
    row = lax.broadcasted_iota(jnp.int32, (tq, tq), 0)
    col = lax.broadcasted_iota(jnp.int32, (tq, tq), 1)
    stats = [scratch[3 * i:3 * i + 3] for i in range(hpar)]
    for h0 in range(0, B_HEADS, hpar):
        qs = [q_ref[0, :, (h0 + i) * QK_DIM:(h0 + i + 1) * QK_DIM] for i in range(hpar)]
        for acc_ref, m_ref, l_ref in stats:
            m_ref[...] = jnp.full_like(m_ref, -jnp.inf)
            l_ref[...] = jnp.zeros_like(l_ref)
            acc_ref[...] = jnp.zeros_like(acc_ref)

        def block(j, diagonal):
            kv = kv_ref[0, pl.ds(pl.multiple_of(j * tq, tq), tq), :]
            ss = [_dot_nt(qh, kv) for qh in qs]
            for s, (acc_ref, m_ref, l_ref) in zip(ss, stats):
                if diagonal:
                    s = jnp.where(col <= row, s, -jnp.inf)
                _softmax_step(s, kv[:, :KV_RANK], m_ref, l_ref, acc_ref)

        def body(j, carry):
            block(j, False)
            return carry

        lax.fori_loop(0, qi, body, 0)
        block(qi, True)
        for i, (acc_ref, m_ref, l_ref) in enumerate(stats):
            h = h0 + i
            o_lat = _softmax_finish(acc_ref[...], l_ref[...]).astype(BF)
            o_ref[0, :, h * V_DIM:(h + 1) * V_DIM] = _dot(o_lat, wuv_ref[h]).astype(o_ref.dtype)


def _attn_prompt(q_cat, kv_cat, wuv, b, t, tq=512, hpar=2):
    q3 = q_cat.reshape(b, t, q_cat.shape[-1])
    kv3 = kv_cat.reshape(b, t, kv_cat.shape[-1])
    body = functools.partial(_attn_p_body, tq=tq, hpar=hpar)
    return pl.pallas_call(
        body,
        grid=(b, t // tq),
        in_specs=[pl.BlockSpec((1, tq, q3.shape[-1]), lambda bi, qi: (bi, qi, 0)),
                  pl.BlockSpec((1, t, QK_DIM), lambda bi, qi: (bi, 0, 0)),
                  pl.BlockSpec(wuv.shape, lambda bi, qi: (0, 0, 0))],
        out_specs=pl.BlockSpec((1, tq, B_HEADS * V_DIM), lambda bi, qi: (bi, qi, 0)),
        out_shape=jax.ShapeDtypeStruct((b, t, B_HEADS * V_DIM), BF),
        scratch_shapes=[pltpu.VMEM((tq, KV_RANK), F32),
                        pltpu.VMEM((tq, LANES), F32),
                        pltpu.VMEM((tq, LANES), F32)] * hpar,
        compiler_params=_params(("arbitrary", "arbitrary")),
        name="attn_prompt",
    )(q3, kv3, wuv)


def _attn_s_body(pt_ref, q_ref, kvn_ref, ckv_hbm, kpe_hbm, wuv_ref, o_ref,
                 ckv_buf, kpe_buf, sem, qs_ref, acc_ref, m_ref, l_ref,
                 *, nch, ppc, n_pages, t_new, nslots):
    b = pl.program_id(0)
    ahead = nslots - 1

    def page_copies(bi, c, p):
        slot = c % nslots
        page = pt_ref[bi * n_pages + c * ppc + p]
        span = pl.ds(p * PAGE_SIZE, PAGE_SIZE)
        return (pltpu.make_async_copy(ckv_hbm.at[page], ckv_buf.at[slot, span, :], sem.at[slot, 0]),
                pltpu.make_async_copy(kpe_hbm.at[page], kpe_buf.at[slot, p], sem.at[slot, 1]))

    def start_all(bi, c):
        for p in range(ppc):
            for cp in page_copies(bi, c, p):
                cp.start()

    def wait_all(bi, c):
        for p in range(ppc):
            for cp in page_copies(bi, c, p):
                cp.wait()

    @pl.when(b == 0)
    def _():
        for c in range(ahead):
            start_all(b, c)

    for h in range(B_HEADS):
        qs_ref[h * t_new:(h + 1) * t_new, :] = q_ref[0, :, h * QK_DIM:(h + 1) * QK_DIM]
    m_ref[...] = jnp.full_like(m_ref, -jnp.inf)
    l_ref[...] = jnp.zeros_like(l_ref)
    acc_ref[...] = jnp.zeros_like(acc_ref)
    qs = qs_ref[...]
    qb = qs.astype(BF)
    q_lat = qb[:, :KV_RANK]
    q_pe = qb[:, KV_RANK:KV_RANK + ROPE_DIM]

    def scores(c):
        wait_all(b, c)
        slot = c % nslots
        kc = ckv_buf[slot].astype(BF)
        kpt = jnp.concatenate([kpe_buf[slot, p] for p in range(ppc)], axis=1).astype(BF)
        return kc, _dot_nt(q_lat, kc) + _dot(q_pe, kpt)

    cur = scores(0)
    for c in range(nch):
        nxt = c + ahead
        if nxt < nch:
            start_all(b, nxt)
        else:
            @pl.when(b + 1 < pl.num_programs(0))
            def _():
                start_all(b + 1, nxt - nch)
        following = scores(c + 1) if c + 1 < nch else None
        _softmax_step(cur[1], cur[0], m_ref, l_ref, acc_ref)
        cur = following

    kn = kvn_ref[0]
    rows = B_HEADS * t_new
    sn = _dot_nt(qs, kn)
    tq_ = lax.broadcasted_iota(jnp.int32, (rows, t_new), 0) % t_new
    tk_ = lax.broadcasted_iota(jnp.int32, (rows, t_new), 1)
    sn = jnp.where(tk_ <= tq_, sn, -jnp.inf)
    m_prev = m_ref[:, 0:1]
    m_new = jnp.maximum(m_prev, jnp.max(sn, axis=-1, keepdims=True))
    alpha = jnp.exp2((m_prev - m_new) * _EXP2_SCALE)
    pn = jnp.exp2((sn - m_new) * _EXP2_SCALE)
    l = alpha * jnp.sum(l_ref[...], axis=-1, keepdims=True) + jnp.sum(pn, axis=-1, keepdims=True)
    acc = alpha * acc_ref[...]
    vn = kn[:, :KV_RANK].astype(BF).astype(F32)
    pn = pn.astype(BF).astype(F32)
    for j in range(t_new):
        acc = acc + pn[:, j:j + 1] * vn[j:j + 1, :]
    o_lat = (acc / l).astype(BF)
    r = _dot(o_lat, wuv_ref[...])
    for h in range(B_HEADS):
        o_ref[0, :, h * V_DIM:(h + 1) * V_DIM] = r[h * t_new:(h + 1) * t_new, h * V_DIM:(h + 1) * V_DIM]


def _attn_sample(page_table, q_cat, kv_cat, cache_ckv, cache_kpe, wuv_all, nb, t_new, ppc=8, nslots=4):
    n_pages = page_table.shape[1]
    nch = n_pages // ppc
    assert nch % nslots == 0
    keys = ppc * PAGE_SIZE
    q3 = q_cat.reshape(nb, t_new, q_cat.shape[-1])
    kv3 = kv_cat.reshape(nb, t_new, kv_cat.shape[-1])
    rows = B_HEADS * t_new
    body = functools.partial(_attn_s_body, nch=nch, ppc=ppc, n_pages=n_pages, t_new=t_new, nslots=nslots)
    grid_spec = pltpu.PrefetchScalarGridSpec(
        num_scalar_prefetch=1,
        grid=(nb,),
        in_specs=[pl.BlockSpec((1, t_new, q3.shape[-1]), lambda bi, pt: (bi, 0, 0)),
                  pl.BlockSpec((1, t_new, QK_DIM), lambda bi, pt: (bi, 0, 0)),
                  pl.BlockSpec(memory_space=pl.ANY),
                  pl.BlockSpec(memory_space=pl.ANY),
                  pl.BlockSpec(wuv_all.shape, lambda bi, pt: (0, 0))],
        out_specs=pl.BlockSpec((1, t_new, B_HEADS * V_DIM), lambda bi, pt: (bi, 0, 0)),
        scratch_shapes=[pltpu.VMEM((nslots, keys, KV_RANK), F32),
                        pltpu.VMEM((nslots, ppc, ROPE_DIM, PAGE_SIZE), F32),
                        pltpu.SemaphoreType.DMA((nslots, 2)),
                        pltpu.VMEM((rows, QK_DIM), F32),
                        pltpu.VMEM((rows, KV_RANK), F32),
                        pltpu.VMEM((rows, LANES), F32),
                        pltpu.VMEM((rows, LANES), F32)])
    return pl.pallas_call(
        body,
        grid_spec=grid_spec,
        out_shape=jax.ShapeDtypeStruct((nb, t_new, B_HEADS * V_DIM), F32),
        compiler_params=_params(("arbitrary",)),
        name="attn_sample",
    )(page_table.reshape(-1), q3, kv3, cache_ckv, cache_kpe, wuv_all)


def _mix_body(oa_ref, ob_ref, ga_ref, gb_ref, x_ref, gt_ref, g_ref, wa_ref, wb_ref, wo_ref, o_ref):
    ya = _dot(oa_ref[...].astype(BF), wa_ref[...])
    yb = _dot(ob_ref[...].astype(BF), wb_ref[...])
    merged = jax.nn.sigmoid(ga_ref[...]) * ya + jax.nn.sigmoid(gb_ref[...]) * yb
    z = _dot(merged.astype(BF), wo_ref[...])
    x = x_ref[...]
    o_ref[...] = x + gt_ref[...] * _rms(z, g_ref[...]).reshape(x.shape)


def _mix(o_a, o_b, proj, x, mod, g, wa, wb, wo, gb, tr):
    bm, r, d = x.shape
    tm = gb * tr
    nr = r // tr
    gate0 = 4 * A_HEADS * A_DK // d

    def rows2(width, cb):
        return pl.BlockSpec((tm, width), lambda a, i: (a * nr + i, cb))

    def const(w):
        return pl.BlockSpec(w.shape, lambda a, i: (0, 0))

    return pl.pallas_call(
        _mix_body,
        grid=(bm // gb, nr),
        in_specs=[rows2(o_a.shape[1], 0), rows2(o_b.shape[1], 0),
                  rows2(d, gate0), rows2(d, gate0 + 1),
                  pl.BlockSpec((gb, tr, d), lambda a, i: (a, i, 0)),
                  pl.BlockSpec((gb, 1, d), lambda a, i: (a, 0, 2)),
                  const(g), const(wa), const(wb), const(wo)],
        out_specs=pl.BlockSpec((gb, tr, d), lambda a, i: (a, i, 0)),
        out_shape=jax.ShapeDtypeStruct(x.shape, F32),
        compiler_params=_params(("arbitrary", "arbitrary")),
        name="mix",
    )(o_a, o_b, proj, proj, x, mod, g, wa, wb, wo)


def _mlp_body(x_ref, sh_ref, sc_ref, gt_ref, g1_ref, g2_ref, wu_ref, wd_ref, o_ref, h_ref, acc_ref):
    f = pl.program_id(2)

    @pl.when(f == 0)
    def _():
        x = x_ref[...]
        h = _rms(x, g1_ref[...]) * (1.0 + sc_ref[...]) + sh_ref[...]
        h_ref[...] = h.reshape(h_ref.shape).astype(BF)
        acc_ref[...] = jnp.zeros_like(acc_ref)

    u = jnp.maximum(_dot(h_ref[...], wu_ref[...]), 0.0)
    acc_ref[...] += _dot((u * u).astype(BF), wd_ref[...])

    @pl.when(f == pl.num_programs(2) - 1)
    def _():
        x = x_ref[...]
        o_ref[...] = x + gt_ref[...] * _rms(acc_ref[...], g2_ref[...]).reshape(x.shape)


def _mlp(x, mod, g1, g2, wu, wd, gb, tr, tf=1024):
    bm, r, d = x.shape
    tm = gb * tr
    nr = r // tr
    ff = wu.shape[1]

    def modspec(piece):
        return pl.BlockSpec((gb, 1, d), lambda a, i, f: (a, 0, piece))

    return pl.pallas_call(
        _mlp_body,
        grid=(bm // gb, nr, ff // tf),
        in_specs=[pl.BlockSpec((gb, tr, d), lambda a, i, f: (a, i, 0)),
                  modspec(3), modspec(4), modspec(5),
                  pl.BlockSpec((1, d), lambda a, i, f: (0, 0)),
                  pl.BlockSpec((1, d), lambda a, i, f: (0, 0)),
                  pl.BlockSpec((d, tf), lambda a, i, f: (0, f)),
                  pl.BlockSpec((tf, d), lambda a, i, f: (f, 0))],
        out_specs=pl.BlockSpec((gb, tr, d), lambda a, i, f: (a, i, 0)),
        out_shape=jax.ShapeDtypeStruct(x.shape, F32),
        scratch_shapes=[pltpu.VMEM((tm, d), BF), pltpu.VMEM((tm, d), F32)],
        compiler_params=_params(("arbitrary", "arbitrary", "arbitrary")),
        name="mlp",
    )(x, mod, mod, mod, g1, g2, wu, wd)


def _rope_tables(pos):
    half = ROPE_DIM // 2
    inv = ROPE_THETA ** (-jnp.arange(half, dtype=F32) / half)
    ang = pos.astype(F32)[:, None] * inv[None, :]
    z = jnp.zeros((pos.shape[0], LANES - ROPE_DIM), F32)
    cos = jnp.cos(ang)
    sin = jnp.sin(ang)
    return jnp.concatenate([cos, cos, z], axis=1), jnp.concatenate([sin, sin, z], axis=1)


def _swap_halves(w):
    half = w.shape[-1] // 2
    return jnp.concatenate([-w[..., half:], w[..., :half]], axis=-1)


def _pad_lanes(w):
    pad = [(0, 0)] * (w.ndim - 1) + [(0, LANES - w.shape[-1])]
    return jnp.pad(w, pad)


def kernel(x_prompt, x_sample, c_prompt, c_sample, cache_ckv, cache_kpe, state_hgrn, page_table, w_ada, b_ada, g_pre_mix, g_post_mix, g_pre_mlp, g_post_mlp, w_in, lb_logits, g_hgrn_norm, w_a_out, g_q_norm, w_q_up, g_kv_norm, w_kv_up, w_b_out, w_o, w_up, w_down):
    depth = w_in.shape[0]
    assert depth == 1
    batch, seq, d = x_prompt.shape
    nb, t_new, _ = x_sample.shape
    past_len = page_table.shape[1] * PAGE_SIZE
    hk = A_HEADS * A_DK

    wit = w_in[0].T
    o_qd = 3 * hk + A_HEADS * A_DV
    o_kpe = o_qd + Q_RANK + KV_RANK
    o_gate = o_kpe + ROPE_DIM
    w_main = jnp.concatenate([wit[:o_qd].astype(BF), wit[o_gate:].astype(BF)], axis=0)
    w_kpe = wit[o_kpe:o_gate]
    half = ROPE_DIM // 2
    w_kpe_swapped = jnp.concatenate([-w_kpe[half:], w_kpe[:half]], axis=0)
    zpad = jnp.zeros((LANES - ROPE_DIM, d), F32)
    w_mla = jnp.concatenate([wit[o_qd:o_kpe], w_kpe, zpad, w_kpe_swapped, zpad], axis=0).astype(BF)
    wq = w_q_up[0].reshape(Q_RANK, B_HEADS, NOPE_DIM + ROPE_DIM)
    wq_pe = wq[..., NOPE_DIM:]
    wq_cat = jnp.concatenate([wq[..., :NOPE_DIM].reshape(Q_RANK, -1),
                              _pad_lanes(wq_pe).reshape(Q_RANK, -1),
                              _pad_lanes(_swap_halves(wq_pe)).reshape(Q_RANK, -1)], axis=1).astype(BF)
    wkv = w_kv_up[0].reshape(KV_RANK, B_HEADS, NOPE_DIM + V_DIM)
    wuk = wkv[..., :NOPE_DIM].transpose(1, 2, 0).astype(BF)
    wuv = wkv[..., NOPE_DIM:].transpose(1, 0, 2).astype(BF)
    wuv_all = wkv[..., NOPE_DIM:].reshape(KV_RANK, B_HEADS * V_DIM).astype(BF)
    wa = w_a_out[0].astype(BF)
    wb = w_b_out[0].astype(BF)
    wo = w_o[0].astype(BF)
    wu = w_up[0].astype(BF)
    wd = w_down[0].astype(BF)

    c_all = jnp.concatenate([c_prompt, c_sample], axis=0)
    mod = _ada(c_all, w_ada[0], b_ada[0][None, :])
    mod = mod.reshape(batch + nb, 1, 6 * d)
    mod_p, mod_s = mod[:batch], mod[batch:]

    cos_p, sin_p = _rope_tables(jnp.arange(seq))
    cos_s, sin_s = _rope_tables(past_len + jnp.arange(t_new))
    gs = _TILE
    reps = gs
    cos_s = jnp.tile(cos_s, (reps, 1))
    sin_s = jnp.tile(sin_s, (reps, 1))

    def layer(x, mod_g, gb, tr, hgrn, attend, cos, sin, n_pos_tiles, act_dtype):
        tm = gb * tr
        proj = _proj(x, mod_g, g_pre_mix, w_main, gb, tr, 2048)
        proj_mla = _proj(x, mod_g, g_pre_mix, w_mla, gb, tr, w_mla.shape[0])
        o_a, s_fin = hgrn(proj)
        q_cat, kv_cat, ckv, kpe = _mla_prep(proj_mla, cos, sin, g_q_norm, g_kv_norm, wq_cat, wuk,
                                            tm, n_pos_tiles, act_dtype)
        o_b = attend(q_cat, kv_cat)
        mix_gb, mix_tr = (1, tr // 2) if gb == 1 else (gb // 2, tr)
        x1 = _mix(o_a.reshape(-1, o_a.shape[-1]), o_b.reshape(-1, o_b.shape[-1]), proj, x, mod_g,
                  g_post_mix, wa, wb, wo, mix_gb, mix_tr)
        y = _mlp(x1, mod_g, g_pre_mlp, g_post_mlp, wu, wd, gb, tr)
        return y, ckv, kpe, s_fin

    kpe_pages_t = jnp.swapaxes(cache_kpe[0], 1, 2)

    tr_p = 512
    y_p, ckv_p, kpe_p, s_p = layer(
        x_prompt, mod_p, 1, tr_p,
        lambda proj: _hgrn_prompt(proj, lb_logits, g_hgrn_norm, batch, seq),
        lambda q, kv: _attn_prompt(q, kv, wuv, batch, seq),
        cos_p, sin_p, seq // tr_p, BF)
    y_s, ckv_s, kpe_s, s_s = layer(
        x_sample, mod_s, gs, t_new,
        lambda proj: _hgrn_sample(proj, lb_logits, g_hgrn_norm, state_hgrn, nb, t_new),
        lambda q, kv: _attn_sample(page_table, q, kv, cache_ckv[0], kpe_pages_t, wuv_all, nb, t_new),
        cos_s, sin_s, 1, F32)

    return (y_p, y_s,
            ckv_p.reshape(1, batch, seq, KV_RANK), kpe_p.reshape(1, batch, seq, ROPE_DIM),
            s_p[None],
            ckv_s.reshape(1, nb, t_new, KV_RANK), kpe_s.reshape(1, nb, t_new, ROPE_DIM),
            s_s)
```

```python
import functools
import math

import numpy as np
import jax
import jax.numpy as jnp
from jax import lax
from jax.experimental import pallas as pl
from jax.experimental.pallas import tpu as pltpu

BF = jnp.bfloat16
F32 = jnp.float32

D_MODEL = 2048
A_HEADS = 8
A_DK = 128
A_DV = 128
A_CHUNK = 64
B_HEADS = 8
Q_RANK = 512
KV_RANK = 512
NOPE_DIM = 128
ROPE_DIM = 64
V_DIM = 128
ROPE_THETA = 10000.0
MLA_SCALE = (NOPE_DIM + ROPE_DIM) ** -0.5
D_FF = 4 * D_MODEL
EPS = 1e-6
PAGE_SIZE = 128

LANES = 128
QK_DIM = KV_RANK + LANES
VMEM_LIMIT = 56 * 1024 * 1024

_NT = (((1,), (1,)), ((), ()))
_TN = (((0,), (0,)), ((), ()))


def _dot(a, b):
    return jnp.dot(a, b, preferred_element_type=F32)


def _dot_nt(a, b):
    return lax.dot_general(a, b, _NT, preferred_element_type=F32)


def _dot_tn(a, b):
    return lax.dot_general(a, b, _TN, preferred_element_type=F32)


def _rms(x, g):
    r = lax.rsqrt(jnp.mean(x * x, axis=-1, keepdims=True) + EPS)
    return (x * r) * g


def _params(sem):
    return pltpu.CompilerParams(dimension_semantics=sem, vmem_limit_bytes=VMEM_LIMIT)


def _ada_body(c_ref, w_ref, b_ref, o_ref):
    c = c_ref[...]
    s = (c * jax.nn.sigmoid(c)).astype(BF)
    o_ref[...] = _dot(s, w_ref[...].astype(BF)) + b_ref[...]


def _ada(c_all, w_ada, b_ada, tn=1024):
    m, d = c_all.shape
    n = w_ada.shape[1]
    return pl.pallas_call(
        _ada_body,
        grid=(n // tn,),
        in_specs=[pl.BlockSpec((m, d), lambda j: (0, 0)),
                  pl.BlockSpec((d, tn), lambda j: (0, j)),
                  pl.BlockSpec((1, tn), lambda j: (0, j))],
        out_specs=pl.BlockSpec((m, tn), lambda j: (0, j)),
        out_shape=jax.ShapeDtypeStruct((m, n), F32),
        compiler_params=_params(("arbitrary",)),
        name="ada",
    )(c_all, w_ada, b_ada)


def _proj_body(x_ref, sh_ref, sc_ref, g_ref, w_ref, o_ref, h_ref):
    @pl.when(pl.program_id(2) == 0)
    def _():
        x = x_ref[...]
        h = _rms(x, g_ref[...]) * (1.0 + sc_ref[...]) + sh_ref[...]
        h_ref[...] = h.reshape(h_ref.shape).astype(BF)

    o_ref[...] = _dot_nt(h_ref[...], w_ref[...])


def _proj(x, mod, g, w_t, gb, tr, tn):
    bm, r, d = x.shape
    n = w_t.shape[0]
    tm = gb * tr
    nr = r // tr
    return pl.pallas_call(
        _proj_body,
        grid=(bm // gb, nr, n // tn),
        in_specs=[pl.BlockSpec((gb, tr, d), lambda a, i, j: (a, i, 0)),
                  pl.BlockSpec((gb, 1, d), lambda a, i, j: (a, 0, 0)),
                  pl.BlockSpec((gb, 1, d), lambda a, i, j: (a, 0, 1)),
                  pl.BlockSpec((1, d), lambda a, i, j: (0, 0)),
                  pl.BlockSpec((tn, d), lambda a, i, j: (j, 0))],
        out_specs=pl.BlockSpec((tm, tn), lambda a, i, j: (a * nr + i, j)),
        out_shape=jax.ShapeDtypeStruct((bm * r, n), F32),
        scratch_shapes=[pltpu.VMEM((tm, d), BF)],
        compiler_params=_params(("arbitrary", "arbitrary", "arbitrary")),
        name="proj",
    )(x, mod, mod, g, w_t)


_TILE = 64


def _hgrn_consts(chunk):
    t = np.arange(_TILE)[:, None]
    u = np.arange(_TILE)[None, :]
    tril = ((t // chunk) == (u // chunk)) & (u <= t)
    masks = []
    m = chunk
    while m >= 2:
        h = m // 2
        masks.append(((t // m) == (u // m)) & ((t % m) >= h) & ((u % m) < h))
        m = h
    masks.append(t == u)
    mk = np.concatenate([a.astype(np.float32) for a in masks], axis=0)
    return jnp.asarray(tril.astype(np.float32), BF), jnp.asarray(mk, F32)


def _split3(x):
    hi = x.astype(BF)
    r1 = x - hi.astype(F32)
    mid = r1.astype(BF)
    lo = (r1 - mid.astype(F32)).astype(BF)
    return hi, mid, lo


def _hgrn_gates(qa, fa, lb):
    q = qa * jax.nn.sigmoid(qa)
    f = lb + (1.0 - lb) * jax.nn.sigmoid(fa)
    return q, jnp.log(f), 1.0 - f


def _hgrn_lb(lbl):
    e = jnp.exp(lbl - jnp.max(lbl, axis=0, keepdims=True))
    return e[0:1] / jnp.sum(e, axis=0, keepdims=True)


def _hgrn_scores(q, k, levels, mk_ref):
    nlev = len(levels)
    sc = jnp.where(mk_ref[nlev * _TILE:(nlev + 1) * _TILE, :] > 0.5,
                   _dot_nt(q.astype(BF), k.astype(BF)), 0.0)
    row = lax.broadcasted_iota(jnp.int32, q.shape, 0)
    for l, e in enumerate(levels):
        m = (1 << nlev) >> l
        x = (jnp.where((row & (m // 2)) != 0, q, k) * e).astype(BF)
        p = _dot_nt(x, x)
        sc = sc + jnp.where(mk_ref[l * _TILE:(l + 1) * _TILE, :] > 0.5, p, 0.0)
    return sc


def _row_bcast(x, rows, n):
    w = x.shape[1]
    return jnp.concatenate([jnp.broadcast_to(x[r:r + 1, :], (n, w)) for r in rows], axis=0)


def _hgrn_decays(tril_ref, logf, chunk):
    w = logf.shape[1]
    a = _dot(tril_ref[...], jnp.concatenate(_split3(logf), axis=1))
    b = (a[:, :w] + a[:, w:2 * w] + a[:, 2 * w:]) * math.log2(math.e)
    sub = lax.broadcasted_iota(jnp.int32, (8, w), 0)

    def refs_in_groups(offs):
        span = 8 // len(offs)
        pieces = []
        for g in range(_TILE // 8):
            rows = [jnp.broadcast_to(b[g * 8 + o:g * 8 + o + 1, :], (8, w)) for o in offs]
            ref = rows[-1]
            for i in range(len(offs) - 2, -1, -1):
                ref = jnp.where(sub < (i + 1) * span, rows[i], ref)
            pieces.append(ref)
        return jnp.concatenate(pieces, axis=0)

    b_last = _row_bcast(b, range(chunk - 1, _TILE, chunk), chunk)
    levels = []
    m = chunk
    while m >= 2:
        if m >= 16:
            ref = _row_bcast(b, range(m // 2, _TILE, m), m)
        else:
            ref = refs_in_groups(list(range(m // 2, 8, m)))
        levels.append(jnp.exp2(-jnp.abs(b - ref)))
        m //= 2
    return jnp.exp2(b), jnp.exp2(b_last - b), levels


def _hgrn_out(o, ga, gn):
    return (_rms(o, gn) * jax.nn.sigmoid(ga)).astype(BF)


def _hgrn_p_body(qa_ref, fa_ref, ia_ref, ga_ref, lb_ref, gn_ref, nm_ref, mk_ref,
                 o_ref, s_ref, st_ref, *, nchunks, chunk_len, nh):
    ti = pl.program_id(2)

    @pl.when(ti == 0)
    def _():
        st_ref[...] = jnp.zeros_like(st_ref)

    lb = _hgrn_lb(lb_ref[...])
    gn = gn_ref[...]

    def chunk(c, carry):
        rows = pl.ds(pl.multiple_of(c * _TILE, _TILE), _TILE)
        qa, fa, ia, ga = (r[0, rows, :] for r in (qa_ref, fa_ref, ia_ref, ga_ref))
        sts = [st_ref[j] for j in range(nh)]
        hs = range(nh)
        cols = [slice(j * A_DK, (j + 1) * A_DK) for j in hs]
        q, logf, k = _hgrn_gates(qa, fa, lb)
        eb, er, levels = _hgrn_decays(nm_ref, logf, chunk_len)
        qe = (q * eb).astype(BF)
        ke = (k * er).astype(BF)
        vs = [ia[:, c].astype(BF) for c in cols]
        o_inter = [_dot_nt(qe[:, c], sts[j].astype(BF)) for j, c in enumerate(cols)]
        scs = [_hgrn_scores(q[:, c], k[:, c], [e[:, c] for e in levels], mk_ref) for c in cols]
        os_ = [o_inter[j] + _dot(scs[j].astype(BF), vs[j]) for j in hs]
        upd = [_dot_tn(vs[j], ke[:, c]) for j, c in enumerate(cols)]
        for j, c in enumerate(cols):
            st_ref[j] = sts[j] * eb[_TILE - 1:_TILE, c] + upd[j]
        o_ref[0, rows, :] = jnp.concatenate(
            [_hgrn_out(os_[j], ga[:, c], gn[:, c]) for j, c in enumerate(cols)], axis=1)
        return carry

    lax.fori_loop(0, nchunks, chunk, 0)

    @pl.when(ti == pl.num_programs(2) - 1)
    def _():
        for j in range(nh):
            s_ref[0, j] = st_ref[j].T


def _hgrn_prompt(proj, lb_logits, g_norm, b, t, tt=512, nh=4):
    nm, mk = _hgrn_consts(A_CHUNK)
    h = A_HEADS
    hg = h // nh
    proj3 = proj.reshape(b, t, proj.shape[-1])

    def col(off):
        return pl.BlockSpec((1, tt, nh * A_DK), lambda bi, hi, ti: (bi, ti, off + hi))

    body = functools.partial(_hgrn_p_body, nchunks=tt // _TILE, chunk_len=A_CHUNK, nh=nh)
    return pl.pallas_call(
        body,
        grid=(b, hg, t // tt),
        in_specs=[col(0), col(hg), col(2 * hg), col(3 * hg),
                  pl.BlockSpec((lb_logits.shape[0], nh * A_DK), lambda bi, hi, ti: (0, hi)),
                  pl.BlockSpec((1, nh * A_DV), lambda bi, hi, ti: (0, hi)),
                  pl.BlockSpec(nm.shape, lambda bi, hi, ti: (0, 0)),
                  pl.BlockSpec(mk.shape, lambda bi, hi, ti: (0, 0))],
        out_specs=[pl.BlockSpec((1, tt, nh * A_DV), lambda bi, hi, ti: (bi, ti, hi)),
                   pl.BlockSpec((1, nh, A_DK, A_DV), lambda bi, hi, ti: (bi, hi, 0, 0))],
        out_shape=[jax.ShapeDtypeStruct((b, t, h * A_DV), BF),
                   jax.ShapeDtypeStruct((b, h, A_DK, A_DV), F32)],
        scratch_shapes=[pltpu.VMEM((nh, A_DV, A_DK), F32)],
        compiler_params=_params(("arbitrary", "arbitrary", "arbitrary")),
        name="hgrn_prompt",
    )(proj3, proj3, proj3, proj3, lb_logits, g_norm, nm, mk)


def _hgrn_s_body(qa_ref, fa_ref, ia_ref, ga_ref, lb_ref, gn_ref, nm_ref, mk_ref, sel_ref,
                 s0_ref, o_ref, s_ref, *, chunk_len, gb, nh):
    lb = _hgrn_lb(lb_ref[...])
    q_all, logf_all, k_all = _hgrn_gates(qa_ref[...], fa_ref[...], lb)
    eb_all, er_all, levels_all = _hgrn_decays(nm_ref, logf_all, chunk_len)
    sel = sel_ref[...]
    selb = sel.astype(BF)
    gn = gn_ref[...]
    outs = []
    for hh in range(nh):
        hc = slice(hh * A_DK, (hh + 1) * A_DK)
        q, logf, k = q_all[:, hc], logf_all[:, hc], k_all[:, hc]
        vf = ia_ref[:, hc]
        v = vf.astype(BF)
        parts = _split3(logf)
        dec = jnp.exp(_dot_tn(jnp.concatenate(parts, axis=0), jnp.concatenate([selb] * len(parts), axis=0)))
        qe = q * eb_all[:, hc]
        q_blk = (jnp.concatenate([qe] * gb, axis=1) * sel).astype(BF)
        v_blk = (jnp.concatenate([vf] * gb, axis=1) * sel).astype(BF)
        s0 = s0_ref[0, :, hh]
        o = _dot(q_blk, s0.reshape(gb * A_DK, A_DV).astype(BF))
        sc = _hgrn_scores(q, k, [e[:, hc] for e in levels_all], mk_ref)
        o = o + _dot(sc.astype(BF), v)
        upd = _dot_tn((k * er_all[:, hc]).astype(BF), v_blk)
        for j in range(gb):
            cols = slice(j * A_DV, (j + 1) * A_DV)
            s_ref[0, j, hh] = dec[:, cols] * s0[j] + upd[:, cols]
        outs.append(_hgrn_out(o, ga_ref[:, hc], gn[:, hc]))
    o_ref[...] = jnp.concatenate(outs, axis=1)


def _hgrn_sample(proj, lb_logits, g_norm, state, nb, t, nh=4):
    gb = _TILE // t
    nm, mk = _hgrn_consts(t)
    h = A_HEADS
    hg = h // nh
    sel = np.zeros((_TILE, gb * A_DV), np.float32)
    for j in range(gb):
        sel[j * t:(j + 1) * t, j * A_DV:(j + 1) * A_DV] = 1.0
    sel = jnp.asarray(sel)

    def col(off):
        return pl.BlockSpec((_TILE, nh * A_DK), lambda i, hi: (i, off + hi))

    body = functools.partial(_hgrn_s_body, chunk_len=t, gb=gb, nh=nh)
    st_spec = pl.BlockSpec((1, gb, nh, A_DK, A_DV), lambda i, hi: (0, i, hi, 0, 0))
    return pl.pallas_call(
        body,
        grid=(nb // gb, hg),
        in_specs=[col(0), col(hg), col(2 * hg), col(3 * hg),
                  pl.BlockSpec((lb_logits.shape[0], nh * A_DK), lambda i, hi: (0, hi)),
                  pl.BlockSpec((1, nh * A_DV), lambda i, hi: (0, hi)),
                  pl.BlockSpec(nm.shape, lambda i, hi: (0, 0)),
                  pl.BlockSpec(mk.shape, lambda i, hi: (0, 0)),
                  pl.BlockSpec(sel.shape, lambda i, hi: (0, 0)),
                  st_spec],
        out_specs=[pl.BlockSpec((_TILE, nh * A_DV), lambda i, hi: (i, hi)), st_spec],
        out_shape=[jax.ShapeDtypeStruct((nb * t, h * A_DV), BF),
                   jax.ShapeDtypeStruct(state.shape, F32)],
        compiler_params=_params(("arbitrary", "arbitrary")),
        name="hgrn_sample",
    )(proj, proj, proj, proj, lb_logits, g_norm, nm, mk, sel, state)


def _mla_prep_body(p_ref, cos_ref, sin_ref, gq_ref, gkv_ref, wq_ref, wuk_ref,
                   q_ref, kv_ref, ckv_ref, kpe_ref):
    p = p_ref[...]
    cos = cos_ref[...]
    sin = sin_ref[...]
    qn = _rms(p[:, :Q_RANK], gq_ref[...]).astype(BF)
    qf = _dot(qn, wq_ref[...])
    hw = B_HEADS * NOPE_DIM
    for h in range(B_HEADS):
        c0 = h * NOPE_DIM
        q_lat = _dot(qf[:, c0:c0 + NOPE_DIM].astype(BF), wuk_ref[h])
        q_pe = qf[:, hw + c0:hw + c0 + LANES] * cos + qf[:, 2 * hw + c0:2 * hw + c0 + LANES] * sin
        q_ref[:, h * QK_DIM:h * QK_DIM + KV_RANK] = q_lat.astype(q_ref.dtype)
        q_ref[:, h * QK_DIM + KV_RANK:(h + 1) * QK_DIM] = q_pe.astype(q_ref.dtype)
    ckv = _rms(p[:, Q_RANK:Q_RANK + KV_RANK], gkv_ref[...])
    o = Q_RANK + KV_RANK
    kpe = p[:, o:o + LANES] * cos + p[:, o + LANES:o + 2 * LANES] * sin
    ckv_ref[...] = ckv
    kpe_ref[...] = kpe[:, :ROPE_DIM]
    kv_ref[:, :KV_RANK] = ckv.astype(kv_ref.dtype)
    kv_ref[:, KV_RANK:] = kpe.astype(kv_ref.dtype)


def _mla_prep(proj_mla, cos, sin, g_q, g_kv, wq, wuk, tm, n_pos_tiles, act_dtype):
    rows, pc = proj_mla.shape
    return pl.pallas_call(
        _mla_prep_body,
        grid=(rows // tm,),
        in_specs=[pl.BlockSpec((tm, pc), lambda i: (i, 0)),
                  pl.BlockSpec((tm, LANES), lambda i: (i % n_pos_tiles, 0)),
                  pl.BlockSpec((tm, LANES), lambda i: (i % n_pos_tiles, 0)),
                  pl.BlockSpec((1, Q_RANK), lambda i: (0, 0)),
                  pl.BlockSpec((1, KV_RANK), lambda i: (0, 0)),
                  pl.BlockSpec(wq.shape, lambda i: (0, 0)),
                  pl.BlockSpec(wuk.shape, lambda i: (0, 0, 0))],
        out_specs=[pl.BlockSpec((tm, B_HEADS * QK_DIM), lambda i: (i, 0)),
                   pl.BlockSpec((tm, QK_DIM), lambda i: (i, 0)),
                   pl.BlockSpec((tm, KV_RANK), lambda i: (i, 0)),
                   pl.BlockSpec((tm, ROPE_DIM), lambda i: (i, 0))],
        out_shape=[jax.ShapeDtypeStruct((rows, B_HEADS * QK_DIM), act_dtype),
                   jax.ShapeDtypeStruct((rows, QK_DIM), act_dtype),
                   jax.ShapeDtypeStruct((rows, KV_RANK), F32),
                   jax.ShapeDtypeStruct((rows, ROPE_DIM), F32)],
        compiler_params=_params(("arbitrary",)),
        name="mla_prep",
    )(proj_mla, cos, sin, g_q, g_kv, wq, wuk)


_EXP2_SCALE = MLA_SCALE * math.log2(math.e)


def _lane_blocks(x):
    return [x[:, k * LANES:(k + 1) * LANES] for k in range(x.shape[1] // LANES)]


def _softmax_step(s, v, m_ref, l_ref, acc_ref, rows=slice(None)):
    blocks = _lane_blocks(s)
    m_prev = m_ref[rows, :]
    m_new = jnp.maximum(m_prev, jnp.max(functools.reduce(jnp.maximum, blocks), axis=-1, keepdims=True))
    m_ref[rows, :] = m_new
    alpha = jnp.exp2((m_prev - m_new) * _EXP2_SCALE)
    ps = [jnp.exp2((blk - m_new) * _EXP2_SCALE) for blk in blocks]
    l_ref[rows, :] = alpha * l_ref[rows, :] + functools.reduce(jnp.add, ps)
    pv = _dot(jnp.concatenate(ps, axis=1).astype(BF), v)
    for k, blk in enumerate(_lane_blocks(pv)):
        cols = slice(k * LANES, (k + 1) * LANES)
        acc_ref[rows, cols] = alpha * acc_ref[rows, cols] + blk


def _softmax_finish(acc, l_part):
    return acc / jnp.sum(l_part, axis=-1, keepdims=True)


def _attn_p_body(q_ref, kv_ref, wuv_ref, o_ref, *scratch, tq, hpar):
    qi = pl.program_id(1)
    half = tq // 2
    stats = [scratch[3 * i:3 * i + 3] for i in range(hpar)]
    for h0 in range(0, B_HEADS, hpar):
        qs = [q_ref[0, :, (h0 + i) * QK_DIM:(h0 + i + 1) * QK_DIM] for i in range(hpar)]
        for acc_ref, m_ref, l_ref in stats:
            m_ref[...] = jnp.full_like(m_ref, -jnp.inf)
            l_ref[...] = jnp.zeros_like(l_ref)
            acc_ref[...] = jnp.zeros_like(acc_ref)

        def body(j, carry):
            kv = kv_ref[0, pl.ds(pl.multiple_of(j * tq, tq), tq), :]
            ss = [_dot_nt(qh, kv) for qh in qs]
            for s, (acc_ref, m_ref, l_ref) in zip(ss, stats):
                _softmax_step(s, kv[:, :KV_RANK], m_ref, l_ref, acc_ref)
            return carry

        lax.fori_loop(0, qi, body, 0)

        kv = kv_ref[0, pl.ds(pl.multiple_of(qi * tq, tq), tq), :]
        parts = [(slice(0, half), half), (slice(half, tq), tq)]
        ss = [[_dot_nt(qh[r], kv[:nk]) for r, nk in parts] for qh in qs]
        for s_parts, (acc_ref, m_ref, l_ref) in zip(ss, stats):
            for (r, nk), s in zip(parts, s_parts):
                qpos = lax.broadcasted_iota(jnp.int32, s.shape, 0) + r.start
                kpos = lax.broadcasted_iota(jnp.int32, s.shape, 1)
                s = jnp.where(kpos <= qpos, s, -jnp.inf)
                _softmax_step(s, kv[:nk, :KV_RANK], m_ref, l_ref, acc_ref, rows=r)
        for i, (acc_ref, m_ref, l_ref) in enumerate(stats):
            h = h0 + i
            o_lat = _softmax_finish(acc_ref[...], l_ref[...]).astype(BF)
            o_ref[0, :, h * V_DIM:(h + 1) * V_DIM] = _dot(o_lat, wuv_ref[h]).astype(o_ref.dtype)


def _attn_prompt(q_cat, kv_cat, wuv, b, t, tq=512, hpar=4):
    q3 = q_cat.reshape(b, t, q_cat.shape[-1])
    kv3 = kv_cat.reshape(b, t, kv_cat.shape[-1])
    body = functools.partial(_attn_p_body, tq=tq, hpar=hpar)
    return pl.pallas_call(
        body,
        grid=(b, t // tq),
        in_specs=[pl.BlockSpec((1, tq, q3.shape[-1]), lambda bi, qi: (bi, qi, 0)),
                  pl.BlockSpec((1, t, QK_DIM), lambda bi, qi: (bi, 0, 0)),
                  pl.BlockSpec(wuv.shape, lambda bi, qi: (0, 0, 0))],
        out_specs=pl.BlockSpec((1, tq, B_HEADS * V_DIM), lambda bi, qi: (bi, qi, 0)),
        out_shape=jax.ShapeDtypeStruct((b, t, B_HEADS * V_DIM), BF),
        scratch_shapes=[pltpu.VMEM((tq, KV_RANK), F32),
                        pltpu.VMEM((tq, LANES), F32),
                        pltpu.VMEM((tq, LANES), F32)] * hpar,
        compiler_params=_params(("arbitrary", "arbitrary")),
        name="attn_prompt",
    )(q3, kv3, wuv)


def _attn_s_body(pt_ref, q_ref, kvn_ref, ckv_hbm, kpe_hbm, wuv_ref, o_ref,
                 ckv_buf, kpe_buf, sem, qs_ref, acc_ref, m_ref, l_ref,
                 *, nch, ppc, n_pages, t_new, nslots):
    b = pl.program_id(0)
    ahead = nslots - 1

    def page_copies(bi, c, p):
        slot = c % nslots
        page = pt_ref[bi * n_pages + c * ppc + p]
        span = pl.ds(p * PAGE_SIZE, PAGE_SIZE)
        return (pltpu.make_async_copy(ckv_hbm.at[page], ckv_buf.at[slot, span, :], sem.at[slot, 0]),
                pltpu.make_async_copy(kpe_hbm.at[page], kpe_buf.at[slot, p], sem.at[slot, 1]))

    def start_all(bi, c):
        for p in range(ppc):
            for cp in page_copies(bi, c, p):
                cp.start()

    def wait_all(bi, c):
        for p in range(ppc):
            for cp in page_copies(bi, c, p):
                cp.wait()

    @pl.when(b == 0)
    def _():
        for c in range(ahead):
            start_all(b, c)

    for h in range(B_HEADS):
        qs_ref[h * t_new:(h + 1) * t_new, :] = q_ref[0, :, h * QK_DIM:(h + 1) * QK_DIM]
    m_ref[...] = jnp.full_like(m_ref, -jnp.inf)
    l_ref[...] = jnp.zeros_like(l_ref)
    acc_ref[...] = jnp.zeros_like(acc_ref)
    qs = qs_ref[...]
    qb = qs.astype(BF)
    q_lat = qb[:, :KV_RANK]
    q_pe = qb[:, KV_RANK:KV_RANK + ROPE_DIM]

    def scores(c):
        wait_all(b, c)
        slot = c % nslots
        kc = ckv_buf[slot].astype(BF)
        kpt = jnp.concatenate([kpe_buf[slot, p] for p in range(ppc)], axis=1).astype(BF)
        return kc, _dot_nt(q_lat, kc) + _dot(q_pe, kpt)

    cur = scores(0)
    for c in range(nch):
        nxt = c + ahead
        if nxt < nch:
            start_all(b, nxt)
        else:
            @pl.when(b + 1 < pl.num_programs(0))
            def _():
                start_all(b + 1, nxt - nch)
        following = scores(c + 1) if c + 1 < nch else None
        _softmax_step(cur[1], cur[0], m_ref, l_ref, acc_ref)
        cur = following

    kn = kvn_ref[0]
    rows = B_HEADS * t_new
    sn = _dot_nt(qs, kn)
    tq_ = lax.broadcasted_iota(jnp.int32, (rows, t_new), 0) % t_new
    tk_ = lax.broadcasted_iota(jnp.int32, (rows, t_new), 1)
    sn = jnp.where(tk_ <= tq_, sn, -jnp.inf)
    m_prev = m_ref[:, 0:1]
    m_new = jnp.maximum(m_prev, jnp.max(sn, axis=-1, keepdims=True))
    alpha = jnp.exp2((m_prev - m_new) * _EXP2_SCALE)
    pn = jnp.exp2((sn - m_new) * _EXP2_SCALE)
    l = alpha * jnp.sum(l_ref[...], axis=-1, keepdims=True) + jnp.sum(pn, axis=-1, keepdims=True)
    acc = alpha * acc_ref[...]
    vn = kn[:, :KV_RANK].astype(BF).astype(F32)
    pn = pn.astype(BF).astype(F32)
    for j in range(t_new):
        acc = acc + pn[:, j:j + 1] * vn[j:j + 1, :]
    o_lat = (acc / l).astype(BF)
    r = _dot(o_lat, wuv_ref[...])
    for h in range(B_HEADS):
        o_ref[0, :, h * V_DIM:(h + 1) * V_DIM] = r[h * t_new:(h + 1) * t_new, h * V_DIM:(h + 1) * V_DIM]


def _attn_sample(page_table, q_cat, kv_cat, cache_ckv, cache_kpe, wuv_all, nb, t_new, ppc=8, nslots=4):
    n_pages = page_table.shape[1]
    nch = n_pages // ppc
    assert nch % nslots == 0
    keys = ppc * PAGE_SIZE
    q3 = q_cat.reshape(nb, t_new, q_cat.shape[-1])
    kv3 = kv_cat.reshape(nb, t_new, kv_cat.shape[-1])
    rows = B_HEADS * t_new
    body = functools.partial(_attn_s_body, nch=nch, ppc=ppc, n_pages=n_pages, t_new=t_new, nslots=nslots)
    grid_spec = pltpu.PrefetchScalarGridSpec(
        num_scalar_prefetch=1,
        grid=(nb,),
        in_specs=[pl.BlockSpec((1, t_new, q3.shape[-1]), lambda bi, pt: (bi, 0, 0)),
                  pl.BlockSpec((1, t_new, QK_DIM), lambda bi, pt: (bi, 0, 0)),
                  pl.BlockSpec(memory_space=pl.ANY),
                  pl.BlockSpec(memory_space=pl.ANY),
                  pl.BlockSpec(wuv_all.shape, lambda bi, pt: (0, 0))],
        out_specs=pl.BlockSpec((1, t_new, B_HEADS * V_DIM), lambda bi, pt: (bi, 0, 0)),
        scratch_shapes=[pltpu.VMEM((nslots, keys, KV_RANK), F32),
                        pltpu.VMEM((nslots, ppc, ROPE_DIM, PAGE_SIZE), F32),
                        pltpu.SemaphoreType.DMA((nslots, 2)),
                        pltpu.VMEM((rows, QK_DIM), F32),
                        pltpu.VMEM((rows, KV_RANK), F32),
                        pltpu.VMEM((rows, LANES), F32),
                        pltpu.VMEM((rows, LANES), F32)])
    return pl.pallas_call(
        body,
        grid_spec=grid_spec,
        out_shape=jax.ShapeDtypeStruct((nb, t_new, B_HEADS * V_DIM), F32),
        compiler_params=_params(("arbitrary",)),
        name="attn_sample",
    )(page_table.reshape(-1), q3, kv3, cache_ckv, cache_kpe, wuv_all)


def _mix_body(oa_ref, ob_ref, ga_ref, gb_ref, x_ref, gt_ref, g_ref, wa_ref, wb_ref, wo_ref, o_ref):
    ya = _dot(oa_ref[...].astype(BF), wa_ref[...])
    yb = _dot(ob_ref[...].astype(BF), wb_ref[...])
    merged = jax.nn.sigmoid(ga_ref[...]) * ya + jax.nn.sigmoid(gb_ref[...]) * yb
    z = _dot(merged.astype(BF), wo_ref[...])
    x = x_ref[...]
    o_ref[...] = x + gt_ref[...] * _rms(z, g_ref[...]).reshape(x.shape)


def _mix(o_a, o_b, proj, x, mod, g, wa, wb, wo, gb, tr):
    bm, r, d = x.shape
    tm = gb * tr
    nr = r // tr
    gate0 = 4 * A_HEADS * A_DK // d

    def rows2(width, cb):
        return pl.BlockSpec((tm, width), lambda a, i: (a * nr + i, cb))

    def const(w):
        return pl.BlockSpec(w.shape, lambda a, i: (0, 0))

    return pl.pallas_call(
        _mix_body,
        grid=(bm // gb, nr),
        in_specs=[rows2(o_a.shape[1], 0), rows2(o_b.shape[1], 0),
                  rows2(d, gate0), rows2(d, gate0 + 1),
                  pl.BlockSpec((gb, tr, d), lambda a, i: (a, i, 0)),
                  pl.BlockSpec((gb, 1, d), lambda a, i: (a, 0, 2)),
                  const(g), const(wa), const(wb), const(wo)],
        out_specs=pl.BlockSpec((gb, tr, d), lambda a, i: (a, i, 0)),
        out_shape=jax.ShapeDtypeStruct(x.shape, F32),
        compiler_params=_params(("arbitrary", "arbitrary")),
        name="mix",
    )(o_a, o_b, proj, proj, x, mod, g, wa, wb, wo)


def _mlp_body(x_ref, sh_ref, sc_ref, gt_ref, g1_ref, g2_ref, wu_ref, wd_ref, o_ref, h_ref, acc_ref):
    f = pl.program_id(2)

    @pl.when(f == 0)
    def _():
        x = x_ref[...]
        h = _rms(x, g1_ref[...]) * (1.0 + sc_ref[...]) + sh_ref[...]
        h_ref[...] = h.reshape(h_ref.shape).astype(BF)
        acc_ref[...] = jnp.zeros_like(acc_ref)

    u = jnp.maximum(_dot(h_ref[...], wu_ref[...]), 0.0)
    acc_ref[...] += _dot((u * u).astype(BF), wd_ref[...])

    @pl.when(f == pl.num_programs(2) - 1)
    def _():
        x = x_ref[...]
        o_ref[...] = x + gt_ref[...] * _rms(acc_ref[...], g2_ref[...]).reshape(x.shape)


def _mlp(x, mod, g1, g2, wu, wd, gb, tr, tf=1024):
    bm, r, d = x.shape
    tm = gb * tr
    nr = r // tr
    ff = wu.shape[1]

    def modspec(piece):
        return pl.BlockSpec((gb, 1, d), lambda a, i, f: (a, 0, piece))

    return pl.pallas_call(
        _mlp_body,
        grid=(bm // gb, nr, ff // tf),
        in_specs=[pl.BlockSpec((gb, tr, d), lambda a, i, f: (a, i, 0)),
                  modspec(3), modspec(4), modspec(5),
                  pl.BlockSpec((1, d), lambda a, i, f: (0, 0)),
                  pl.BlockSpec((1, d), lambda a, i, f: (0, 0)),
                  pl.BlockSpec((d, tf), lambda a, i, f: (0, f)),
                  pl.BlockSpec((tf, d), lambda a, i, f: (f, 0))],
        out_specs=pl.BlockSpec((gb, tr, d), lambda a, i, f: (a, i, 0)),
        out_shape=jax.ShapeDtypeStruct(x.shape, F32),
        scratch_shapes=[pltpu.VMEM((tm, d), BF), pltpu.VMEM((tm, d), F32)],
        compiler_params=_params(("arbitrary", "arbitrary", "arbitrary")),
        name="mlp",
    )(x, mod, mod, mod, g1, g2, wu, wd)


def _rope_tables(pos):
    half = ROPE_DIM // 2
    inv = ROPE_THETA ** (-jnp.arange(half, dtype=F32) / half)
    ang = pos.astype(F32)[:, None] * inv[None, :]
    z = jnp.zeros((pos.shape[0], LANES - ROPE_DIM), F32)
    cos = jnp.cos(ang)
    sin = jnp.sin(ang)
    return jnp.concatenate([cos, cos, z], axis=1), jnp.concatenate([sin, sin, z], axis=1)


def _swap_halves(w):
    half = w.shape[-1] // 2
    return jnp.concatenate([-w[..., half:], w[..., :half]], axis=-1)


def _pad_lanes(w):
    pad = [(0, 0)] * (w.ndim - 1) + [(0, LANES - w.shape[-1])]
    return jnp.pad(w, pad)


def kernel(x_prompt, x_sample, c_prompt, c_sample, cache_ckv, cache_kpe, state_hgrn, page_table, w_ada, b_ada, g_pre_mix, g_post_mix, g_pre_mlp, g_post_mlp, w_in, lb_logits, g_hgrn_norm, w_a_out, g_q_norm, w_q_up, g_kv_norm, w_kv_up, w_b_out, w_o, w_up, w_down):
    depth = w_in.shape[0]
    assert depth == 1
    batch, seq, d = x_prompt.shape
    nb, t_new, _ = x_sample.shape
    past_len = page_table.shape[1] * PAGE_SIZE
    hk = A_HEADS * A_DK

    wit = w_in[0].T
    o_qd = 3 * hk + A_HEADS * A_DV
    o_kpe = o_qd + Q_RANK + KV_RANK
    o_gate = o_kpe + ROPE_DIM
    w_main = jnp.concatenate([wit[:o_qd].astype(BF), wit[o_gate:].astype(BF)], axis=0)
    w_kpe = wit[o_kpe:o_gate]
    half = ROPE_DIM // 2
    w_kpe_swapped = jnp.concatenate([-w_kpe[half:], w_kpe[:half]], axis=0)
    zpad = jnp.zeros((LANES - ROPE_DIM, d), F32)
    w_mla = jnp.concatenate([wit[o_qd:o_kpe], w_kpe, zpad, w_kpe_swapped, zpad], axis=0).astype(BF)
    wq = w_q_up[0].reshape(Q_RANK, B_HEADS, NOPE_DIM + ROPE_DIM)
    wq_pe = wq[..., NOPE_DIM:]
    wq_cat = jnp.concatenate([wq[..., :NOPE_DIM].reshape(Q_RANK, -1),
                              _pad_lanes(wq_pe).reshape(Q_RANK, -1),
                              _pad_lanes(_swap_halves(wq_pe)).reshape(Q_RANK, -1)], axis=1).astype(BF)
    wkv = w_kv_up[0].reshape(KV_RANK, B_HEADS, NOPE_DIM + V_DIM)
    wuk = wkv[..., :NOPE_DIM].transpose(1, 2, 0).astype(BF)
    wuv = wkv[..., NOPE_DIM:].transpose(1, 0, 2).astype(BF)
    wuv_all = wkv[..., NOPE_DIM:].reshape(KV_RANK, B_HEADS * V_DIM).astype(BF)
    wa = w_a_out[0].astype(BF)
    wb = w_b_out[0].astype(BF)
    wo = w_o[0].astype(BF)
    wu = w_up[0].astype(BF)
    wd = w_down[0].astype(BF)

    c_all = jnp.concatenate([c_prompt, c_sample], axis=0)
    mod = _ada(c_all, w_ada[0], b_ada[0][None, :])
    mod = mod.reshape(batch + nb, 1, 6 * d)
    mod_p, mod_s = mod[:batch], mod[batch:]

    cos_p, sin_p = _rope_tables(jnp.arange(seq))
    cos_s, sin_s = _rope_tables(past_len + jnp.arange(t_new))
    gs = _TILE
    reps = gs
    cos_s = jnp.tile(cos_s, (reps, 1))
    sin_s = jnp.tile(sin_s, (reps, 1))

    def layer(x, mod_g, gb, tr, hgrn, attend, cos, sin, n_pos_tiles, act_dtype):
        tm = gb * tr
        proj = _proj(x, mod_g, g_pre_mix, w_main, gb, tr, 2048)
        proj_mla = _proj(x, mod_g, g_pre_mix, w_mla, gb, tr, w_mla.shape[0])
        o_a, s_fin = hgrn(proj)
        q_cat, kv_cat, ckv, kpe = _mla_prep(proj_mla, cos, sin, g_q_norm, g_kv_norm, wq_cat, wuk,
                                            tm, n_pos_tiles, act_dtype)
        o_b = attend(q_cat, kv_cat)
        mix_gb, mix_tr = (1, tr // 2) if gb == 1 else (gb // 2, tr)
        x1 = _mix(o_a.reshape(-1, o_a.shape[-1]), o_b.reshape(-1, o_b.shape[-1]), proj, x, mod_g,
                  g_post_mix, wa, wb, wo, mix_gb, mix_tr)
        y = _mlp(x1, mod_g, g_pre_mlp, g_post_mlp, wu, wd, gb, tr)
        return y, ckv, kpe, s_fin

    kpe_pages_t = jnp.swapaxes(cache_kpe[0], 1, 2)

    tr_p = 512
    y_p, ckv_p, kpe_p, s_p = layer(
        x_prompt, mod_p, 1, tr_p,
        lambda proj: _hgrn_prompt(proj, lb_logits, g_hgrn_norm, batch, seq),
        lambda q, kv: _attn_prompt(q, kv, wuv, batch, seq),
        cos_p, sin_p, seq // tr_p, BF)
    y_s, ckv_s, kpe_s, s_s = layer(
        x_sample, mod_s, gs, t_new,
        lambda proj: _hgrn_sample(proj, lb_logits, g_hgrn_norm, state_hgrn, nb, t_new),
        lambda q, kv: _attn_sample(page_table, q, kv, cache_ckv[0], kpe_pages_t, wuv_all, nb, t_new),
        cos_s, sin_s, 1, F32)

    return (y_p, y_s,
            ckv_p.reshape(1, batch, seq, KV_RANK), kpe_p.reshape(1, batch, seq, ROPE_DIM),
            s_p[None],
            ckv_s.reshape(1, nb, t_new, KV_RANK), kpe_s.reshape(1, nb, t_new, ROPE_DIM),
            s_s)
```

```python
import functools
import math

import numpy as np
import jax
import jax.numpy as jnp
from jax import lax
from jax.experimental import pallas as pl
from jax.experimental.pallas import tpu as pltpu

BF = jnp.bfloat16
F32 = jnp.float32

D_MODEL = 2048
A_HEADS = 8
A_DK = 128
A_DV = 128
A_CHUNK = 64
B_HEADS = 8
Q_RANK = 512
KV_RANK = 512
NOPE_DIM = 128
ROPE_DIM = 64
V_DIM = 128
ROPE_THETA = 10000.0
MLA_SCALE = (NOPE_DIM + ROPE_DIM) ** -0.5
D_FF = 4 * D_MODEL
EPS = 1e-6
PAGE_SIZE = 128

LANES = 128
QK_DIM = KV_RANK + LANES
VMEM_LIMIT = 56 * 1024 * 1024

_NT = (((1,), (1,)), ((), ()))
_TN = (((0,), (0,)), ((), ()))


def _dot(a, b):
    return jnp.dot(a, b, preferred_element_type=F32)


def _dot_nt(a, b):
    return lax.dot_general(a, b, _NT, preferred_element_type=F32)


def _dot_tn(a, b):
    return lax.dot_general(a, b, _TN, preferred_element_type=F32)


def _rms(x, g):
    r = lax.rsqrt(jnp.mean(x * x, axis=-1, keepdims=True) + EPS)
    return (x * r) * g


def _params(sem):
    return pltpu.CompilerParams(dimension_semantics=sem, vmem_limit_bytes=VMEM_LIMIT)


def _ada_body(c_ref, w_ref, b_ref, o_ref):
    c = c_ref[...]
    s = (c * jax.nn.sigmoid(c)).astype(BF)
    o_ref[...] = _dot(s, w_ref[...].astype(BF)) + b_ref[...]


def _ada(c_all, w_ada, b_ada, tn=1024):
    m, d = c_all.shape
    n = w_ada.shape[1]
    return pl.pallas_call(
        _ada_body,
        grid=(n // tn,),
        in_specs=[pl.BlockSpec((m, d), lambda j: (0, 0)),
                  pl.BlockSpec((d, tn), lambda j: (0, j)),
                  pl.BlockSpec((1, tn), lambda j: (0, j))],
        out_specs=pl.BlockSpec((m, tn), lambda j: (0, j)),
        out_shape=jax.ShapeDtypeStruct((m, n), F32),
        compiler_params=_params(("arbitrary",)),
        name="ada",
    )(c_all, w_ada, b_ada)


def _proj_body(x_ref, sh_ref, sc_ref, g_ref, w_ref, o_ref, h_ref):
    @pl.when(pl.program_id(2) == 0)
    def _():
        x = x_ref[...]
        h = _rms(x, g_ref[...]) * (1.0 + sc_ref[...]) + sh_ref[...]
        h_ref[...] = h.reshape(h_ref.shape).astype(BF)

    o_ref[...] = _dot_nt(h_ref[...], w_ref[...])


def _proj(x, mod, g, w_t, gb, tr, tn, skip=None):
    bm, r, d = x.shape
    tm = gb * tr
    nr = r // tr
    if skip is None:
        n = w_t.shape[0]
        w_spec = pl.BlockSpec((tn, d), lambda a, i, j: (j, 0))
    else:
        first, gap = skip[0] // tn, skip[1]
        n = w_t.shape[0] - gap
        w_spec = pl.BlockSpec((pl.Element(tn), pl.Element(d)),
                              lambda a, i, j: (pl.multiple_of(j * tn + jnp.where(j >= first, gap, 0),
                                                              math.gcd(tn, gap)), 0))
    return pl.pallas_call(
        _proj_body,
        grid=(bm // gb, nr, n // tn),
        in_specs=[pl.BlockSpec((gb, tr, d), lambda a, i, j: (a, i, 0)),
                  pl.BlockSpec((gb, 1, d), lambda a, i, j: (a, 0, 0)),
                  pl.BlockSpec((gb, 1, d), lambda a, i, j: (a, 0, 1)),
                  pl.BlockSpec((1, d), lambda a, i, j: (0, 0)),
                  w_spec],
        out_specs=pl.BlockSpec((tm, tn), lambda a, i, j: (a * nr + i, j)),
        out_shape=jax.ShapeDtypeStruct((bm * r, n), F32),
        scratch_shapes=[pltpu.VMEM((tm, d), BF)],
        compiler_params=_params(("arbitrary", "arbitrary", "arbitrary")),
        name="proj",
    )(x, mod, mod, g, w_t)


_TILE = 64


def _hgrn_consts(chunk):
    t = np.arange(_TILE)[:, None]
    u = np.arange(_TILE)[None, :]
    tril = ((t // chunk) == (u // chunk)) & (u <= t)
    masks = []
    m = chunk
    while m >= 2:
        h = m // 2
        masks.append(((t // m) == (u // m)) & ((t % m) >= h) & ((u % m) < h))
        m = h
    masks.append(t == u)
    mk = np.concatenate([a.astype(np.float32) for a in masks], axis=0)
    return jnp.asarray(tril.astype(np.float32), BF), jnp.asarray(mk, F32)


def _split3(x):
    hi = x.astype(BF)
    r1 = x - hi.astype(F32)
    mid = r1.astype(BF)
    lo = (r1 - mid.astype(F32)).astype(BF)
    return hi, mid, lo


def _hgrn_gates(qa, fa, lb):
    q = qa * jax.nn.sigmoid(qa)
    f = lb + (1.0 - lb) * jax.nn.sigmoid(fa)
    return q, jnp.log(f), 1.0 - f


def _hgrn_lb(lbl):
    e = jnp.exp(lbl - jnp.max(lbl, axis=0, keepdims=True))
    return e[0:1] / jnp.sum(e, axis=0, keepdims=True)


def _hgrn_scores(q, k, levels, mk_ref):
    nlev = len(levels)
    sc = jnp.where(mk_ref[nlev * _TILE:(nlev + 1) * _TILE, :] > 0.5,
                   _dot_nt(q.astype(BF), k.astype(BF)), 0.0)
    row = lax.broadcasted_iota(jnp.int32, q.shape, 0)
    for l, e in enumerate(levels):
        m = (1 << nlev) >> l
        x = (jnp.where((row & (m // 2)) != 0, q, k) * e).astype(BF)
        p = _dot_nt(x, x)
        sc = sc + jnp.where(mk_ref[l * _TILE:(l + 1) * _TILE, :] > 0.5, p, 0.0)
    return sc


def _row_bcast(x, rows, n):
    w = x.shape[1]
    return jnp.concatenate([jnp.broadcast_to(x[r:r + 1, :], (n, w)) for r in rows], axis=0)


def _hgrn_decays(tril_ref, logf, chunk):
    w = logf.shape[1]
    a = _dot(tril_ref[...], jnp.concatenate(_split3(logf), axis=1))
    b = (a[:, :w] + a[:, w:2 * w] + a[:, 2 * w:]) * math.log2(math.e)
    sub = lax.broadcasted_iota(jnp.int32, (8, w), 0)

    def refs_in_groups(offs):
        span = 8 // len(offs)
        pieces = []
        for g in range(_TILE // 8):
            rows = [jnp.broadcast_to(b[g * 8 + o:g * 8 + o + 1, :], (8, w)) for o in offs]
            ref = rows[-1]
            for i in range(len(offs) - 2, -1, -1):
                ref = jnp.where(sub < (i + 1) * span, rows[i], ref)
            pieces.append(ref)
        return jnp.concatenate(pieces, axis=0)

    b_last = _row_bcast(b, range(chunk - 1, _TILE, chunk), chunk)
    levels = []
    m = chunk
    while m >= 2:
        if m >= 16:
            ref = _row_bcast(b, range(m // 2, _TILE, m), m)
        else:
            ref = refs_in_groups(list(range(m // 2, 8, m)))
        levels.append(jnp.exp2(-jnp.abs(b - ref)))
        m //= 2
    return jnp.exp2(b), jnp.exp2(b_last - b), levels


def _hgrn_out(o, ga, gn):
    return (_rms(o, gn) * jax.nn.sigmoid(ga)).astype(BF)


def _hgrn_p_body(qa_ref, fa_ref, ia_ref, ga_ref, lb_ref, gn_ref, nm_ref, mk_ref,
                 o_ref, s_ref, st_ref, *, nchunks, chunk_len, nh):
    ti = pl.program_id(2)

    @pl.when(ti == 0)
    def _():
        st_ref[...] = jnp.zeros_like(st_ref)

    lb = _hgrn_lb(lb_ref[...])
    gn = gn_ref[...]

    def chunk(c, carry):
        rows = pl.ds(pl.multiple_of(c * _TILE, _TILE), _TILE)
        qa, fa, ia, ga = (r[0, rows, :] for r in (qa_ref, fa_ref, ia_ref, ga_ref))
        sts = [st_ref[j] for j in range(nh)]
        hs = range(nh)
        cols = [slice(j * A_DK, (j + 1) * A_DK) for j in hs]
        q, logf, k = _hgrn_gates(qa, fa, lb)
        eb, er, levels = _hgrn_decays(nm_ref, logf, chunk_len)
        qe = (q * eb).astype(BF)
        ke = (k * er).astype(BF)
        vs = [ia[:, c].astype(BF) for c in cols]
        o_inter = [_dot_nt(qe[:, c], sts[j].astype(BF)) for j, c in enumerate(cols)]
        scs = [_hgrn_scores(q[:, c], k[:, c], [e[:, c] for e in levels], mk_ref) for c in cols]
        os_ = [o_inter[j] + _dot(scs[j].astype(BF), vs[j]) for j in hs]
        upd = [_dot_tn(vs[j], ke[:, c]) for j, c in enumerate(cols)]
        for j, c in enumerate(cols):
            st_ref[j] = sts[j] * eb[_TILE - 1:_TILE, c] + upd[j]
        o_ref[0, rows, :] = jnp.concatenate(
            [_hgrn_out(os_[j], ga[:, c], gn[:, c]) for j, c in enumerate(cols)], axis=1)
        return carry

    lax.fori_loop(0, nchunks, chunk, 0)

    @pl.when(ti == pl.num_programs(2) - 1)
    def _():
        for j in range(nh):
            s_ref[0, j] = st_ref[j].T


def _hgrn_prompt(proj, lb_logits, g_norm, b, t, tt=512, nh=4):
    nm, mk = _hgrn_consts(A_CHUNK)
    h = A_HEADS
    hg = h // nh
    proj3 = proj.reshape(b, t, proj.shape[-1])

    def col(off):
        return pl.BlockSpec((1, tt, nh * A_DK), lambda bi, hi, ti: (bi, ti, off + hi))

    body = functools.partial(_hgrn_p_body, nchunks=tt // _TILE, chunk_len=A_CHUNK, nh=nh)
    return pl.pallas_call(
        body,
        grid=(b, hg, t // tt),
        in_specs=[col(0), col(hg), col(2 * hg), col(3 * hg),
                  pl.BlockSpec((lb_logits.shape[0], nh * A_DK), lambda bi, hi, ti: (0, hi)),
                  pl.BlockSpec((1, nh * A_DV), lambda bi, hi, ti: (0, hi)),
                  pl.BlockSpec(nm.shape, lambda bi, hi, ti: (0, 0)),
                  pl.BlockSpec(mk.shape, lambda bi, hi, ti: (0, 0))],
        out_specs=[pl.BlockSpec((1, tt, nh * A_DV), lambda bi, hi, ti: (bi, ti, hi)),
                   pl.BlockSpec((1, nh, A_DK, A_DV), lambda bi, hi, ti: (bi, hi, 0, 0))],
        out_shape=[jax.ShapeDtypeStruct((b, t, h * A_DV), BF),
                   jax.ShapeDtypeStruct((b, h, A_DK, A_DV), F32)],
        scratch_shapes=[pltpu.VMEM((nh, A_DV, A_DK), F32)],
        compiler_params=_params(("arbitrary", "arbitrary", "arbitrary")),
        name="hgrn_prompt",
    )(proj3, proj3, proj3, proj3, lb_logits, g_norm, nm, mk)


def _hgrn_s_body(qa_ref, fa_ref, ia_ref, ga_ref, lb_ref, gn_ref, nm_ref, mk_ref, sel_ref,
                 s0_ref, o_ref, s_ref, *, chunk_len, gb, nh):
    lb = _hgrn_lb(lb_ref[...])
    q_all, logf_all, k_all = _hgrn_gates(qa_ref[...], fa_ref[...], lb)
    eb_all, er_all, levels_all = _hgrn_decays(nm_ref, logf_all, chunk_len)
    sel = sel_ref[...]
    selb = sel.astype(BF)
    gn = gn_ref[...]
    outs = []
    for hh in range(nh):
        hc = slice(hh * A_DK, (hh + 1) * A_DK)
        q, logf, k = q_all[:, hc], logf_all[:, hc], k_all[:, hc]
        vf = ia_ref[:, hc]
        v = vf.astype(BF)
        parts = _split3(logf)
        dec = jnp.exp(_dot_tn(jnp.concatenate(parts, axis=0), jnp.concatenate([selb] * len(parts), axis=0)))
        qe = q * eb_all[:, hc]
        q_blk = (jnp.concatenate([qe] * gb, axis=1) * sel).astype(BF)
        v_blk = (jnp.concatenate([vf] * gb, axis=1) * sel).astype(BF)
        s0 = s0_ref[0, :, hh]
        o = _dot(q_blk, s0.reshape(gb * A_DK, A_DV).astype(BF))
        sc = _hgrn_scores(q, k, [e[:, hc] for e in levels_all], mk_ref)
        o = o + _dot(sc.astype(BF), v)
        upd = _dot_tn((k * er_all[:, hc]).astype(BF), v_blk)
        for j in range(gb):
            cols = slice(j * A_DV, (j + 1) * A_DV)
            s_ref[0, j, hh] = dec[:, cols] * s0[j] + upd[:, cols]
        outs.append(_hgrn_out(o, ga_ref[:, hc], gn[:, hc]))
    o_ref[...] = jnp.concatenate(outs, axis=1)


def _hgrn_sample(proj, lb_logits, g_norm, state, nb, t, nh=4):
    gb = _TILE // t
    nm, mk = _hgrn_consts(t)
    h = A_HEADS
    hg = h // nh
    sel = np.zeros((_TILE, gb * A_DV), np.float32)
    for j in range(gb):
        sel[j * t:(j + 1) * t, j * A_DV:(j + 1) * A_DV] = 1.0
    sel = jnp.asarray(sel)

    def col(off):
        return pl.BlockSpec((_TILE, nh * A_DK), lambda i, hi: (i, off + hi))

    body = functools.partial(_hgrn_s_body, chunk_len=t, gb=gb, nh=nh)
    st_spec = pl.BlockSpec((1, gb, nh, A_DK, A_DV), lambda i, hi: (0, i, hi, 0, 0))
    return pl.pallas_call(
        body,
        grid=(nb // gb, hg),
        in_specs=[col(0), col(hg), col(2 * hg), col(3 * hg),
                  pl.BlockSpec((lb_logits.shape[0], nh * A_DK), lambda i, hi: (0, hi)),
                  pl.BlockSpec((1, nh * A_DV), lambda i, hi: (0, hi)),
                  pl.BlockSpec(nm.shape, lambda i, hi: (0, 0)),
                  pl.BlockSpec(mk.shape, lambda i, hi: (0, 0)),
                  pl.BlockSpec(sel.shape, lambda i, hi: (0, 0)),
                  st_spec],
        out_specs=[pl.BlockSpec((_TILE, nh * A_DV), lambda i, hi: (i, hi)), st_spec],
        out_shape=[jax.ShapeDtypeStruct((nb * t, h * A_DV), BF),
                   jax.ShapeDtypeStruct(state.shape, F32)],
        compiler_params=_params(("arbitrary", "arbitrary")),
        name="hgrn_sample",
    )(proj, proj, proj, proj, lb_logits, g_norm, nm, mk, sel, state)


def _mla_prep_body(p_ref, cos_ref, sin_ref, gq_ref, gkv_ref, wq_ref, wuk_ref,
                   q_ref, kv_ref, ckv_ref, kpe_ref):
    p = p_ref[...]
    cos = cos_ref[...]
    sin = sin_ref[...]
    qn = _rms(p[:, :Q_RANK], gq_ref[...]).astype(BF)
    qf = _dot(qn, wq_ref[...])
    hw = B_HEADS * NOPE_DIM
    for h in range(B_HEADS):
        c0 = h * NOPE_DIM
        q_lat = _dot(qf[:, c0:c0 + NOPE_DIM].astype(BF), wuk_ref[h])
        q_pe = qf[:, hw + c0:hw + c0 + LANES] * cos + qf[:, 2 * hw + c0:2 * hw + c0 + LANES] * sin
        q_ref[:, h * QK_DIM:h * QK_DIM + KV_RANK] = q_lat.astype(q_ref.dtype)
        q_ref[:, h * QK_DIM + KV_RANK:(h + 1) * QK_DIM] = q_pe.astype(q_ref.dtype)
    ckv = _rms(p[:, Q_RANK:Q_RANK + KV_RANK], gkv_ref[...])
    o = Q_RANK + KV_RANK
    kpe = p[:, o:o + LANES] * cos + p[:, o + LANES:o + 2 * LANES] * sin
    ckv_ref[...] = ckv
    kpe_ref[...] = kpe[:, :ROPE_DIM]
    kv_ref[:, :KV_RANK] = ckv.astype(kv_ref.dtype)
    kv_ref[:, KV_RANK:] = kpe.astype(kv_ref.dtype)


def _mla_prep(proj_mla, cos, sin, g_q, g_kv, wq, wuk, tm, n_pos_tiles, act_dtype):
    rows, pc = proj_mla.shape
    return pl.pallas_call(
        _mla_prep_body,
        grid=(rows // tm,),
        in_specs=[pl.BlockSpec((tm, pc), lambda i: (i, 0)),
                  pl.BlockSpec((tm, LANES), lambda i: (i % n_pos_tiles, 0)),
                  pl.BlockSpec((tm, LANES), lambda i: (i % n_pos_tiles, 0)),
                  pl.BlockSpec((1, Q_RANK), lambda i: (0, 0)),
                  pl.BlockSpec((1, KV_RANK), lambda i: (0, 0)),
                  pl.BlockSpec(wq.shape, lambda i: (0, 0)),
                  pl.BlockSpec(wuk.shape, lambda i: (0, 0, 0))],
        out_specs=[pl.BlockSpec((tm, B_HEADS * QK_DIM), lambda i: (i, 0)),
                   pl.BlockSpec((tm, QK_DIM), lambda i: (i, 0)),
                   pl.BlockSpec((tm, KV_RANK), lambda i: (i, 0)),
                   pl.BlockSpec((tm, ROPE_DIM), lambda i: (i, 0))],
        out_shape=[jax.ShapeDtypeStruct((rows, B_HEADS * QK_DIM), act_dtype),
                   jax.ShapeDtypeStruct((rows, QK_DIM), act_dtype),
                   jax.ShapeDtypeStruct((rows, KV_RANK), F32),
                   jax.ShapeDtypeStruct((rows, ROPE_DIM), F32)],
        compiler_params=_params(("arbitrary",)),
        name="mla_prep",
    )(proj_mla, cos, sin, g_q, g_kv, wq, wuk)


_EXP2_SCALE = MLA_SCALE * math.log2(math.e)


def _lane_blocks(x):
    return [x[:, k * LANES:(k + 1) * LANES] for k in range(x.shape[1] // LANES)]


def _softmax_step(s, v, m_ref, l_ref, acc_ref, rows=slice(None)):
    blocks = _lane_blocks(s)
    m_prev = m_ref[rows, :]
    m_new = jnp.maximum(m_prev, jnp.max(functools.reduce(jnp.maximum, blocks), axis=-1, keepdims=True))
    m_ref[rows, :] = m_new
    alpha = jnp.exp2((m_prev - m_new) * _EXP2_SCALE)
    ps = [jnp.exp2((blk - m_new) * _EXP2_SCALE) for blk in blocks]
    l_ref[rows, :] = alpha * l_ref[rows, :] + functools.reduce(jnp.add, ps)
    pv = _dot(jnp.concatenate(ps, axis=1).astype(BF), v)
    for k, blk in enumerate(_lane_blocks(pv)):
        cols = slice(k * LANES, (k + 1) * LANES)
        acc_ref[rows, cols] = alpha * acc_ref[rows, cols] + blk


def _softmax_finish(acc, l_part):
    return acc / jnp.sum(l_part, axis=-1, keepdims=True)


def _attn_p_body(q_ref, kv_ref, wuv_ref, o_ref, *scratch, tq, hpar):
    qi = pl.program_id(1)
    half = tq // 2
    stats = [scratch[3 * i:3 * i + 3] for i in range(hpar)]
    for h0 in range(0, B_HEADS, hpar):
        qs = [q_ref[0, :, (h0 + i) * QK_DIM:(h0 + i + 1) * QK_DIM] for i in range(hpar)]
        for acc_ref, m_ref, l_ref in stats:
            m_ref[...] = jnp.full_like(m_ref, -jnp.inf)
            l_ref[...] = jnp.zeros_like(l_ref)
            acc_ref[...] = jnp.zeros_like(acc_ref)

        def body(j, carry):
            kv = kv_ref[0, pl.ds(pl.multiple_of(j * tq, tq), tq), :]
            ss = [_dot_nt(qh, kv) for qh in qs]
            for s, (acc_ref, m_ref, l_ref) in zip(ss, stats):
                _softmax_step(s, kv[:, :KV_RANK], m_ref, l_ref, acc_ref)
            return carry

        lax.fori_loop(0, qi, body, 0)

        kv = kv_ref[0, pl.ds(pl.multiple_of(qi * tq, tq), tq), :]
        parts = [(slice(0, half), half), (slice(half, tq), tq)]
        ss = [[_dot_nt(qh[r], kv[:nk]) for r, nk in parts] for qh in qs]
        for s_parts, (acc_ref, m_ref, l_ref) in zip(ss, stats):
            for (r, nk), s in zip(parts, s_parts):
                qpos = lax.broadcasted_iota(jnp.int32, s.shape, 0) + r.start
                kpos = lax.broadcasted_iota(jnp.int32, s.shape, 1)
                s = jnp.where(kpos <= qpos, s, -jnp.inf)
                _softmax_step(s, kv[:nk, :KV_RANK], m_ref, l_ref, acc_ref, rows=r)
        for i, (acc_ref, m_ref, l_ref) in enumerate(stats):
            h = h0 + i
            o_lat = _softmax_finish(acc_ref[...], l_ref[...]).astype(BF)
            o_ref[0, :, h * V_DIM:(h + 1) * V_DIM] = _dot(o_lat, wuv_ref[h]).astype(o_ref.dtype)


def _attn_prompt(q_cat, kv_cat, wuv, b, t, tq=512, hpar=4):
    q3 = q_cat.reshape(b, t, q_cat.shape[-1])
    kv3 = kv_cat.reshape(b, t, kv_cat.shape[-1])
    body = functools.partial(_attn_p_body, tq=tq, hpar=hpar)
    return pl.pallas_call(
        body,
        grid=(b, t // tq),
        in_specs=[pl.BlockSpec((1, tq, q3.shape[-1]), lambda bi, qi: (bi, qi, 0)),
                  pl.BlockSpec((1, t, QK_DIM), lambda bi, qi: (bi, 0, 0)),
                  pl.BlockSpec(wuv.shape, lambda bi, qi: (0, 0, 0))],
        out_specs=pl.BlockSpec((1, tq, B_HEADS * V_DIM), lambda bi, qi: (bi, qi, 0)),
        out_shape=jax.ShapeDtypeStruct((b, t, B_HEADS * V_DIM), BF),
        scratch_shapes=[pltpu.VMEM((tq, KV_RANK), F32),
                        pltpu.VMEM((tq, LANES), F32),
                        pltpu.VMEM((tq, LANES), F32)] * hpar,
        compiler_params=_params(("arbitrary", "arbitrary")),
        name="attn_prompt",
    )(q3, kv3, wuv)


def _attn_s_body(pt_ref, q_ref, kvn_ref, ckv_hbm, kpe_hbm, wuv_ref, o_ref,
                 ckv_buf, kpe_buf, sem, qs_ref, acc_ref, m_ref, l_ref,
                 *, nch, ppc, n_pages, t_new, nslots):
    b = pl.program_id(0)
    ahead = nslots - 1

    def page_copies(bi, c, p):
        slot = c % nslots
        page = pt_ref[bi * n_pages + c * ppc + p]
        span = pl.ds(p * PAGE_SIZE, PAGE_SIZE)
        return (pltpu.make_async_copy(ckv_hbm.at[page], ckv_buf.at[slot, span, :], sem.at[slot, 0]),
                pltpu.make_async_copy(kpe_hbm.at[page], kpe_buf.at[slot, p], sem.at[slot, 1]))

    def start_all(bi, c):
        for p in range(ppc):
            for cp in page_copies(bi, c, p):
                cp.start()

    def wait_all(bi, c):
        for p in range(ppc):
            for cp in page_copies(bi, c, p):
                cp.wait()

    @pl.when(b == 0)
    def _():
        for c in range(ahead):
            start_all(b, c)

    for h in range(B_HEADS):
        qs_ref[h * t_new:(h + 1) * t_new, :] = q_ref[0, :, h * QK_DIM:(h + 1) * QK_DIM]
    m_ref[...] = jnp.full_like(m_ref, -jnp.inf)
    l_ref[...] = jnp.zeros_like(l_ref)
    acc_ref[...] = jnp.zeros_like(acc_ref)
    qs = qs_ref[...]
    qb = qs.astype(BF)
    q_lat = qb[:, :KV_RANK]
    q_pe = qb[:, KV_RANK:KV_RANK + ROPE_DIM]

    def scores(c):
        wait_all(b, c)
        slot = c % nslots
        kc = ckv_buf[slot].astype(BF)
        kpt = jnp.concatenate([kpe_buf[slot, p] for p in range(ppc)], axis=1).astype(BF)
        return kc, _dot_nt(q_lat, kc) + _dot(q_pe, kpt)

    cur = scores(0)
    for c in range(nch):
        nxt = c + ahead
        if nxt < nch:
            start_all(b, nxt)
        else:
            @pl.when(b + 1 < pl.num_programs(0))
            def _():
                start_all(b + 1, nxt - nch)
        following = scores(c + 1) if c + 1 < nch else None
        _softmax_step(cur[1], cur[0], m_ref, l_ref, acc_ref)
        cur = following

    kn = kvn_ref[0]
    rows = B_HEADS * t_new
    sn = _dot_nt(qs, kn)
    tq_ = lax.broadcasted_iota(jnp.int32, (rows, t_new), 0) % t_new
    tk_ = lax.broadcasted_iota(jnp.int32, (rows, t_new), 1)
    sn = jnp.where(tk_ <= tq_, sn, -jnp.inf)
    m_prev = m_ref[:, 0:1]
    m_new = jnp.maximum(m_prev, jnp.max(sn, axis=-1, keepdims=True))
    alpha = jnp.exp2((m_prev - m_new) * _EXP2_SCALE)
    pn = jnp.exp2((sn - m_new) * _EXP2_SCALE)
    l = alpha * jnp.sum(l_ref[...], axis=-1, keepdims=True) + jnp.sum(pn, axis=-1, keepdims=True)
    acc = alpha * acc_ref[...]
    vn = kn[:, :KV_RANK].astype(BF).astype(F32)
    pn = pn.astype(BF).astype(F32)
    for j in range(t_new):
        acc = acc + pn[:, j:j + 1] * vn[j:j + 1, :]
    o_lat = (acc / l).astype(BF)
    r = _dot(o_lat, wuv_ref[...])
    for h in range(B_HEADS):
        o_ref[0, :, h * V_DIM:(h + 1) * V_DIM] = r[h * t_new:(h + 1) * t_new, h * V_DIM:(h + 1) * V_DIM]


def _attn_sample(page_table, q_cat, kv_cat, cache_ckv, cache_kpe, wuv_all, nb, t_new, ppc=16, nslots=4):
    n_pages = page_table.shape[1]
    nch = n_pages // ppc
    assert nch % nslots == 0
    keys = ppc * PAGE_SIZE
    q3 = q_cat.reshape(nb, t_new, q_cat.shape[-1])
    kv3 = kv_cat.reshape(nb, t_new, kv_cat.shape[-1])
    rows = B_HEADS * t_new
    body = functools.partial(_attn_s_body, nch=nch, ppc=ppc, n_pages=n_pages, t_new=t_new, nslots=nslots)
    grid_spec = pltpu.PrefetchScalarGridSpec(
        num_scalar_prefetch=1,
        grid=(nb,),
        in_specs=[pl.BlockSpec((1, t_new, q3.shape[-1]), lambda bi, pt: (bi, 0, 0)),
                  pl.BlockSpec((1, t_new, QK_DIM), lambda bi, pt: (bi, 0, 0)),
                  pl.BlockSpec(memory_space=pl.ANY),
                  pl.BlockSpec(memory_space=pl.ANY),
                  pl.BlockSpec(wuv_all.shape, lambda bi, pt: (0, 0))],
        out_specs=pl.BlockSpec((1, t_new, B_HEADS * V_DIM), lambda bi, pt: (bi, 0, 0)),
        scratch_shapes=[pltpu.VMEM((nslots, keys, KV_RANK), F32),
                        pltpu.VMEM((nslots, ppc, ROPE_DIM, PAGE_SIZE), F32),
                        pltpu.SemaphoreType.DMA((nslots, 2)),
                        pltpu.VMEM((rows, QK_DIM), F32),
                        pltpu.VMEM((rows, KV_RANK), F32),
                        pltpu.VMEM((rows, LANES), F32),
                        pltpu.VMEM((rows, LANES), F32)])
    return pl.pallas_call(
        body,
        grid_spec=grid_spec,
        out_shape=jax.ShapeDtypeStruct((nb, t_new, B_HEADS * V_DIM), F32),
        compiler_params=_params(("arbitrary",)),
        name="attn_sample",
    )(page_table.reshape(-1), q3, kv3, cache_ckv, cache_kpe, wuv_all)


def _mix_body(oa_ref, ob_ref, ga_ref, gb_ref, x_ref, gt_ref, g_ref, wa_ref, wb_ref, wo_ref, o_ref):
    ya = _dot(oa_ref[...].astype(BF), wa_ref[...])
    yb = _dot(ob_ref[...].astype(BF), wb_ref[...])
    merged = jax.nn.sigmoid(ga_ref[...]) * ya + jax.nn.sigmoid(gb_ref[...]) * yb
    z = _dot(merged.astype(BF), wo_ref[...])
    x = x_ref[...]
    o_ref[...] = x + gt_ref[...] * _rms(z, g_ref[...]).reshape(x.shape)


def _mix(o_a, o_b, proj, x, mod, g, wa, wb, wo, gb, tr):
    bm, r, d = x.shape
    tm = gb * tr
    nr = r // tr
    gate0 = 4 * A_HEADS * A_DK // d

    def rows2(width, cb):
        return pl.BlockSpec((tm, width), lambda a, i: (a * nr + i, cb))

    def const(w):
        return pl.BlockSpec(w.shape, lambda a, i: (0, 0))

    return pl.pallas_call(
        _mix_body,
        grid=(bm // gb, nr),
        in_specs=[rows2(o_a.shape[1], 0), rows2(o_b.shape[1], 0),
                  rows2(d, gate0), rows2(d, gate0 + 1),
                  pl.BlockSpec((gb, tr, d), lambda a, i: (a, i, 0)),
                  pl.BlockSpec((gb, 1, d), lambda a, i: (a, 0, 2)),
                  const(g), const(wa), const(wb), const(wo)],
        out_specs=pl.BlockSpec((gb, tr, d), lambda a, i: (a, i, 0)),
        out_shape=jax.ShapeDtypeStruct(x.shape, F32),
        compiler_params=_params(("arbitrary", "arbitrary")),
        name="mix",
    )(o_a, o_b, proj, proj, x, mod, g, wa, wb, wo)


def _mlp_body(x_ref, sh_ref, sc_ref, gt_ref, g1_ref, g2_ref, wu_ref, wd_ref, o_ref, h_ref, acc_ref):
    f = pl.program_id(2)

    @pl.when(f == 0)
    def _():
        x = x_ref[...]
        h = _rms(x, g1_ref[...]) * (1.0 + sc_ref[...]) + sh_ref[...]
        h_ref[...] = h.reshape(h_ref.shape).astype(BF)
        acc_ref[...] = jnp.zeros_like(acc_ref)

    u = jnp.maximum(_dot(h_ref[...], wu_ref[...]), 0.0)
    acc_ref[...] += _dot((u * u).astype(BF), wd_ref[...])

    @pl.when(f == pl.num_programs(2) - 1)
    def _():
        x = x_ref[...]
        o_ref[...] = x + gt_ref[...] * _rms(acc_ref[...], g2_ref[...]).reshape(x.shape)


def _mlp(x, mod, g1, g2, wu, wd, gb, tr, tf=1024):
    bm, r, d = x.shape
    tm = gb * tr
    nr = r // tr
    ff = wu.shape[1]

    def modspec(piece):
        return pl.BlockSpec((gb, 1, d), lambda a, i, f: (a, 0, piece))

    return pl.pallas_call(
        _mlp_body,
        grid=(bm // gb, nr, ff // tf),
        in_specs=[pl.BlockSpec((gb, tr, d), lambda a, i, f: (a, i, 0)),
                  modspec(3), modspec(4), modspec(5),
                  pl.BlockSpec((1, d), lambda a, i, f: (0, 0)),
                  pl.BlockSpec((1, d), lambda a, i, f: (0, 0)),
                  pl.BlockSpec((d, tf), lambda a, i, f: (0, f)),
                  pl.BlockSpec((tf, d), lambda a, i, f: (f, 0))],
        out_specs=pl.BlockSpec((gb, tr, d), lambda a, i, f: (a, i, 0)),
        out_shape=jax.ShapeDtypeStruct(x.shape, F32),
        scratch_shapes=[pltpu.VMEM((tm, d), BF), pltpu.VMEM((tm, d), F32)],
        compiler_params=_params(("arbitrary", "arbitrary", "arbitrary")),
        name="mlp",
    )(x, mod, mod, mod, g1, g2, wu, wd)


def _rope_tables(pos):
    half = ROPE_DIM // 2
    inv = ROPE_THETA ** (-jnp.arange(half, dtype=F32) / half)
    ang = pos.astype(F32)[:, None] * inv[None, :]
    z = jnp.zeros((pos.shape[0], LANES - ROPE_DIM), F32)
    cos = jnp.cos(ang)
    sin = jnp.sin(ang)
    return jnp.concatenate([cos, cos, z], axis=1), jnp.concatenate([sin, sin, z], axis=1)


def _swap_halves(w):
    half = w.shape[-1] // 2
    return jnp.concatenate([-w[..., half:], w[..., :half]], axis=-1)


def _pad_lanes(w):
    pad = [(0, 0)] * (w.ndim - 1) + [(0, LANES - w.shape[-1])]
    return jnp.pad(w, pad)


def kernel(x_prompt, x_sample, c_prompt, c_sample, cache_ckv, cache_kpe, state_hgrn, page_table, w_ada, b_ada, g_pre_mix, g_post_mix, g_pre_mlp, g_post_mlp, w_in, lb_logits, g_hgrn_norm, w_a_out, g_q_norm, w_q_up, g_kv_norm, w_kv_up, w_b_out, w_o, w_up, w_down):
    depth = w_in.shape[0]
    assert depth == 1
    batch, seq, d = x_prompt.shape
    nb, t_new, _ = x_sample.shape
    past_len = page_table.shape[1] * PAGE_SIZE
    hk = A_HEADS * A_DK

    wit = w_in[0].T
    o_qd = 3 * hk + A_HEADS * A_DV
    o_kpe = o_qd + Q_RANK + KV_RANK
    o_gate = o_kpe + ROPE_DIM
    wit = wit.astype(BF)
    w_kpe = wit[o_kpe:o_gate]
    half = ROPE_DIM // 2
    w_kpe_swapped = jnp.concatenate([-w_kpe[half:], w_kpe[:half]], axis=0)
    zpad = jnp.zeros((LANES - ROPE_DIM, d), BF)
    w_mla = jnp.concatenate([wit[o_qd:o_kpe], w_kpe, zpad, w_kpe_swapped, zpad], axis=0)
    wq = w_q_up[0].reshape(Q_RANK, B_HEADS, NOPE_DIM + ROPE_DIM)
    wq_pe = wq[..., NOPE_DIM:]
    wq_cat = jnp.concatenate([wq[..., :NOPE_DIM].reshape(Q_RANK, -1),
                              _pad_lanes(wq_pe).reshape(Q_RANK, -1),
                              _pad_lanes(_swap_halves(wq_pe)).reshape(Q_RANK, -1)], axis=1).astype(BF)
    wkv = w_kv_up[0].reshape(KV_RANK, B_HEADS, NOPE_DIM + V_DIM)
    wuk = wkv[..., :NOPE_DIM].transpose(1, 2, 0).astype(BF)
    wuv = wkv[..., NOPE_DIM:].transpose(1, 0, 2).astype(BF)
    wuv_all = wkv[..., NOPE_DIM:].reshape(KV_RANK, B_HEADS * V_DIM).astype(BF)
    wa = w_a_out[0].astype(BF)
    wb = w_b_out[0].astype(BF)
    wo = w_o[0].astype(BF)
    wu = w_up[0].astype(BF)
    wd = w_down[0].astype(BF)

    c_all = jnp.concatenate([c_prompt, c_sample], axis=0)
    mod = _ada(c_all, w_ada[0], b_ada[0][None, :])
    mod = mod.reshape(batch + nb, 1, 6 * d)
    mod_p, mod_s = mod[:batch], mod[batch:]

    cos_p, sin_p = _rope_tables(jnp.arange(seq))
    cos_s, sin_s = _rope_tables(past_len + jnp.arange(t_new))
    gs = _TILE
    reps = gs
    cos_s = jnp.tile(cos_s, (reps, 1))
    sin_s = jnp.tile(sin_s, (reps, 1))

    def layer(x, mod_g, gb, tr, hgrn, attend, cos, sin, n_pos_tiles, act_dtype):
        tm = gb * tr
        proj = _proj(x, mod_g, g_pre_mix, wit, gb, tr, 2048, skip=(o_qd, o_gate - o_qd))
        proj_mla = _proj(x, mod_g, g_pre_mix, w_mla, gb, tr, w_mla.shape[0])
        o_a, s_fin = hgrn(proj)
        q_cat, kv_cat, ckv, kpe = _mla_prep(proj_mla, cos, sin, g_q_norm, g_kv_norm, wq_cat, wuk,
                                            tm, n_pos_tiles, act_dtype)
        o_b = attend(q_cat, kv_cat)
        mix_gb, mix_tr = (1, tr // 2) if gb == 1 else (gb // 2, tr)
        x1 = _mix(o_a.reshape(-1, o_a.shape[-1]), o_b.reshape(-1, o_b.shape[-1]), proj, x, mod_g,
                  g_post_mix, wa, wb, wo, mix_gb, mix_tr)
        y = _mlp(x1, mod_g, g_pre_mlp, g_post_mlp, wu, wd, gb, tr)
        return y, ckv, kpe, s_fin

    kpe_pages_t = jnp.swapaxes(cache_kpe[0], 1, 2)

    tr_p = 512
    y_p, ckv_p, kpe_p, s_p = layer(
        x_prompt, mod_p, 1, tr_p,
        lambda proj: _hgrn_prompt(proj, lb_logits, g_hgrn_norm, batch, seq),
        lambda q, kv: _attn_prompt(q, kv, wuv, batch, seq),
        cos_p, sin_p, seq // tr_p, BF)
    y_s, ckv_s, kpe_s, s_s = layer(
        x_sample, mod_s, gs, t_new,
        lambda proj: _hgrn_sample(proj, lb_logits, g_hgrn_norm, state_hgrn, nb, t_new),
        lambda q, kv: _attn_sample(page_table, q, kv, cache_ckv[0], kpe_pages_t, wuv_all, nb, t_new),
        cos_s, sin_s, 1, F32)

    return (y_p, y_s,
            ckv_p.reshape(1, batch, seq, KV_RANK), kpe_p.reshape(1, batch, seq, ROPE_DIM),
            s_p[None],
            ckv_s.reshape(1, nb, t_new, KV_RANK), kpe_s.reshape(1, nb, t_new, ROPE_DIM),
            s_s)
```

```python
import functools
import math

import numpy as np
import jax
import jax.numpy as jnp
from jax import lax
from jax.experimental import pallas as pl
from jax.experimental.pallas import tpu as pltpu

BF = jnp.bfloat16
F32 = jnp.float32

D_MODEL = 2048
A_HEADS = 8
A_DK = 128
A_DV = 128
A_CHUNK = 64
B_HEADS = 8
Q_RANK = 512
KV_RANK = 512
NOPE_DIM = 128
ROPE_DIM = 64
V_DIM = 128
ROPE_THETA = 10000.0
MLA_SCALE = (NOPE_DIM + ROPE_DIM) ** -0.5
D_FF = 4 * D_MODEL
EPS = 1e-6
PAGE_SIZE = 128

LANES = 128
QK_DIM = KV_RANK + LANES
VMEM_LIMIT = 56 * 1024 * 1024

_NT = (((1,), (1,)), ((), ()))
_TN = (((0,), (0,)), ((), ()))


def _dot(a, b):
    return jnp.dot(a, b, preferred_element_type=F32)


def _dot_nt(a, b):
    return lax.dot_general(a, b, _NT, preferred_element_type=F32)


def _dot_tn(a, b):
    return lax.dot_general(a, b, _TN, preferred_element_type=F32)


def _rms(x, g):
    r = lax.rsqrt(jnp.mean(x * x, axis=-1, keepdims=True) + EPS)
    return (x * r) * g


def _params(sem):
    return pltpu.CompilerParams(dimension_semantics=sem, vmem_limit_bytes=VMEM_LIMIT)


def _ada_body(c_ref, w_ref, b_ref, o_ref):
    c = c_ref[...]
    s = (c * jax.nn.sigmoid(c)).astype(BF)
    o_ref[...] = _dot(s, w_ref[...].astype(BF)) + b_ref[...]


def _ada(c_all, w_ada, b_ada, tn=1024):
    m, d = c_all.shape
    n = w_ada.shape[1]
    return pl.pallas_call(
        _ada_body,
        grid=(n // tn,),
        in_specs=[pl.BlockSpec((m, d), lambda j: (0, 0)),
                  pl.BlockSpec((d, tn), lambda j: (0, j)),
                  pl.BlockSpec((1, tn), lambda j: (0, j))],
        out_specs=pl.BlockSpec((m, tn), lambda j: (0, j)),
        out_shape=jax.ShapeDtypeStruct((m, n), F32),
        compiler_params=_params(("arbitrary",)),
        name="ada",
    )(c_all, w_ada, b_ada)


def _proj_body(x_ref, sh_ref, sc_ref, g_ref, w_ref, wx_ref, o_ref, ox_ref, h_ref, *, n_main):
    j = pl.program_id(2)

    @pl.when(j == 0)
    def _():
        x = x_ref[...]
        h = _rms(x, g_ref[...]) * (1.0 + sc_ref[...]) + sh_ref[...]
        h_ref[...] = h.reshape(h_ref.shape).astype(BF)

    @pl.when(j < n_main)
    def _():
        o_ref[...] = _dot_nt(h_ref[...], w_ref[...])

    @pl.when(j == n_main)
    def _():
        ox_ref[...] = _dot_nt(h_ref[...], wx_ref[...])


def _proj(x, mod, g, w_t, skip, w_extra, gb, tr, tn):
    bm, r, d = x.shape
    tm = gb * tr
    nr = r // tr
    first, gap = skip[0] // tn, skip[1]
    n = w_t.shape[0] - gap
    n_main = n // tn
    nx = w_extra.shape[0]

    def w_rows(a, i, j):
        jm = jnp.minimum(j, n_main - 1)
        return pl.multiple_of(jm * tn + jnp.where(jm >= first, gap, 0), math.gcd(tn, gap)), 0

    body = functools.partial(_proj_body, n_main=n_main)
    return pl.pallas_call(
        body,
        grid=(bm // gb, nr, n_main + 1),
        in_specs=[pl.BlockSpec((gb, tr, d), lambda a, i, j: (a, i, 0)),
                  pl.BlockSpec((gb, 1, d), lambda a, i, j: (a, 0, 0)),
                  pl.BlockSpec((gb, 1, d), lambda a, i, j: (a, 0, 1)),
                  pl.BlockSpec((1, d), lambda a, i, j: (0, 0)),
                  pl.BlockSpec((pl.Element(tn), pl.Element(d)), w_rows),
                  pl.BlockSpec((nx, d), lambda a, i, j: (0, 0))],
        out_specs=[pl.BlockSpec((tm, tn), lambda a, i, j: (a * nr + i, jnp.minimum(j, n_main - 1))),
                   pl.BlockSpec((tm, nx), lambda a, i, j: (a * nr + i, 0))],
        out_shape=[jax.ShapeDtypeStruct((bm * r, n), F32),
                   jax.ShapeDtypeStruct((bm * r, nx), F32)],
        scratch_shapes=[pltpu.VMEM((tm, d), BF)],
        compiler_params=_params(("arbitrary", "arbitrary", "arbitrary")),
        name="proj",
    )(x, mod, mod, g, w_t, w_extra)


_TILE = 64


def _hgrn_consts(chunk):
    t = np.arange(_TILE)[:, None]
    u = np.arange(_TILE)[None, :]
    tril = ((t // chunk) == (u // chunk)) & (u <= t)
    masks = []
    m = chunk
    while m >= 2:
        h = m // 2
        masks.append(((t // m) == (u // m)) & ((t % m) >= h) & ((u % m) < h))
        m = h
    masks.append(t == u)
    mk = np.concatenate([a.astype(np.float32) for a in masks], axis=0)
    return jnp.asarray(tril.astype(np.float32), BF), jnp.asarray(mk, F32)


def _split3(x):
    hi = x.astype(BF)
    r1 = x - hi.astype(F32)
    mid = r1.astype(BF)
    lo = (r1 - mid.astype(F32)).astype(BF)
    return hi, mid, lo


def _hgrn_gates(qa, fa, lb):
    q = qa * jax.nn.sigmoid(qa)
    f = lb + (1.0 - lb) * jax.nn.sigmoid(fa)
    return q, jnp.log(f), 1.0 - f


def _hgrn_lb(lbl):
    e = jnp.exp(lbl - jnp.max(lbl, axis=0, keepdims=True))
    return e[0:1] / jnp.sum(e, axis=0, keepdims=True)


def _hgrn_scores(q, k, levels, mk_ref):
    nlev = len(levels)
    sc = jnp.where(mk_ref[nlev * _TILE:(nlev + 1) * _TILE, :] > 0.5,
                   _dot_nt(q.astype(BF), k.astype(BF)), 0.0)
    row = lax.broadcasted_iota(jnp.int32, q.shape, 0)
    for l, e in enumerate(levels):
        m = (1 << nlev) >> l
        x = (jnp.where((row & (m // 2)) != 0, q, k) * e).astype(BF)
        p = _dot_nt(x, x)
        sc = sc + jnp.where(mk_ref[l * _TILE:(l + 1) * _TILE, :] > 0.5, p, 0.0)
    return sc


def _row_bcast(x, rows, n):
    w = x.shape[1]
    return jnp.concatenate([jnp.broadcast_to(x[r:r + 1, :], (n, w)) for r in rows], axis=0)


def _hgrn_decays(tril_ref, logf, chunk):
    w = logf.shape[1]
    a = _dot(tril_ref[...], jnp.concatenate(_split3(logf), axis=1))
    b = (a[:, :w] + a[:, w:2 * w] + a[:, 2 * w:]) * math.log2(math.e)
    sub = lax.broadcasted_iota(jnp.int32, (8, w), 0)

    def refs_in_groups(offs):
        span = 8 // len(offs)
        pieces = []
        for g in range(_TILE // 8):
            rows = [jnp.broadcast_to(b[g * 8 + o:g * 8 + o + 1, :], (8, w)) for o in offs]
            ref = rows[-1]
            for i in range(len(offs) - 2, -1, -1):
                ref = jnp.where(sub < (i + 1) * span, rows[i], ref)
            pieces.append(ref)
        return jnp.concatenate(pieces, axis=0)

    b_last = _row_bcast(b, range(chunk - 1, _TILE, chunk), chunk)
    levels = []
    m = chunk
    while m >= 2:
        if m >= 16:
            ref = _row_bcast(b, range(m // 2, _TILE, m), m)
        else:
            ref = refs_in_groups(list(range(m // 2, 8, m)))
        levels.append(jnp.exp2(-jnp.abs(b - ref)))
        m //= 2
    return jnp.exp2(b), jnp.exp2(b_last - b), levels


def _hgrn_out(o, ga, gn):
    return (_rms(o, gn) * jax.nn.sigmoid(ga)).astype(BF)


def _hgrn_p_body(qa_ref, fa_ref, ia_ref, ga_ref, lb_ref, gn_ref, nm_ref, mk_ref,
                 o_ref, s_ref, st_ref, *, nchunks, chunk_len, nh):
    ti = pl.program_id(2)

    @pl.when(ti == 0)
    def _():
        st_ref[...] = jnp.zeros_like(st_ref)

    lb = _hgrn_lb(lb_ref[...])
    gn = gn_ref[...]

    def chunk(c, carry):
        rows = pl.ds(pl.multiple_of(c * _TILE, _TILE), _TILE)
        qa, fa, ia, ga = (r[0, rows, :] for r in (qa_ref, fa_ref, ia_ref, ga_ref))
        sts = [st_ref[j] for j in range(nh)]
        hs = range(nh)
        cols = [slice(j * A_DK, (j + 1) * A_DK) for j in hs]
        q, logf, k = _hgrn_gates(qa, fa, lb)
        eb, er, levels = _hgrn_decays(nm_ref, logf, chunk_len)
        qe = (q * eb).astype(BF)
        ke = (k * er).astype(BF)
        vs = [ia[:, c].astype(BF) for c in cols]
        o_inter = [_dot_nt(qe[:, c], sts[j].astype(BF)) for j, c in enumerate(cols)]
        scs = [_hgrn_scores(q[:, c], k[:, c], [e[:, c] for e in levels], mk_ref) for c in cols]
        os_ = [o_inter[j] + _dot(scs[j].astype(BF), vs[j]) for j in hs]
        upd = [_dot_tn(vs[j], ke[:, c]) for j, c in enumerate(cols)]
        for j, c in enumerate(cols):
            st_ref[j] = sts[j] * eb[_TILE - 1:_TILE, c] + upd[j]
        o_ref[0, rows, :] = jnp.concatenate(
            [_hgrn_out(os_[j], ga[:, c], gn[:, c]) for j, c in enumerate(cols)], axis=1)
        return carry

    lax.fori_loop(0, nchunks, chunk, 0)

    @pl.when(ti == pl.num_programs(2) - 1)
    def _():
        for j in range(nh):
            s_ref[0, j] = st_ref[j].T


def _hgrn_prompt(proj, lb_logits, g_norm, b, t, tt=512, nh=4):
    nm, mk = _hgrn_consts(A_CHUNK)
    h = A_HEADS
    hg = h // nh
    proj3 = proj.reshape(b, t, proj.shape[-1])

    def col(off):
        return pl.BlockSpec((1, tt, nh * A_DK), lambda bi, hi, ti: (bi, ti, off + hi))

    body = functools.partial(_hgrn_p_body, nchunks=tt // _TILE, chunk_len=A_CHUNK, nh=nh)
    return pl.pallas_call(
        body,
        grid=(b, hg, t // tt),
        in_specs=[col(0), col(hg), col(2 * hg), col(3 * hg),
                  pl.BlockSpec((lb_logits.shape[0], nh * A_DK), lambda bi, hi, ti: (0, hi)),
                  pl.BlockSpec((1, nh * A_DV), lambda bi, hi, ti: (0, hi)),
                  pl.BlockSpec(nm.shape, lambda bi, hi, ti: (0, 0)),
                  pl.BlockSpec(mk.shape, lambda bi, hi, ti: (0, 0))],
        out_specs=[pl.BlockSpec((1, tt, nh * A_DV), lambda bi, hi, ti: (bi, ti, hi)),
                   pl.BlockSpec((1, nh, A_DK, A_DV), lambda bi, hi, ti: (bi, hi, 0, 0))],
        out_shape=[jax.ShapeDtypeStruct((b, t, h * A_DV), BF),
                   jax.ShapeDtypeStruct((b, h, A_DK, A_DV), F32)],
        scratch_shapes=[pltpu.VMEM((nh, A_DV, A_DK), F32)],
        compiler_params=_params(("arbitrary", "arbitrary", "arbitrary")),
        name="hgrn_prompt",
    )(proj3, proj3, proj3, proj3, lb_logits, g_norm, nm, mk)


def _hgrn_s_body(qa_ref, fa_ref, ia_ref, ga_ref, lb_ref, gn_ref, nm_ref, mk_ref, sel_ref,
                 s0_ref, o_ref, s_ref, *, chunk_len, gb, nh):
    lb = _hgrn_lb(lb_ref[...])
    q_all, logf_all, k_all = _hgrn_gates(qa_ref[...], fa_ref[...], lb)
    eb_all, er_all, levels_all = _hgrn_decays(nm_ref, logf_all, chunk_len)
    sel = sel_ref[...]
    selb = sel.astype(BF)
    gn = gn_ref[...]
    outs = []
    for hh in range(nh):
        hc = slice(hh * A_DK, (hh + 1) * A_DK)
        q, logf, k = q_all[:, hc], logf_all[:, hc], k_all[:, hc]
        vf = ia_ref[:, hc]
        v = vf.astype(BF)
        parts = _split3(logf)
        dec = jnp.exp(_dot_tn(jnp.concatenate(parts, axis=0), jnp.concatenate([selb] * len(parts), axis=0)))
        qe = q * eb_all[:, hc]
        q_blk = (jnp.concatenate([qe] * gb, axis=1) * sel).astype(BF)
        v_blk = (jnp.concatenate([vf] * gb, axis=1) * sel).astype(BF)
        s0 = s0_ref[0, :, hh]
        o = _dot(q_blk, s0.reshape(gb * A_DK, A_DV).astype(BF))
        sc = _hgrn_scores(q, k, [e[:, hc] for e in levels_all], mk_ref)
        o = o + _dot(sc.astype(BF), v)
        upd = _dot_tn((k * er_all[:, hc]).astype(BF), v_blk)
        for j in range(gb):
            cols = slice(j * A_DV, (j + 1) * A_DV)
            s_ref[0, j, hh] = dec[:, cols] * s0[j] + upd[:, cols]
        outs.append(_hgrn_out(o, ga_ref[:, hc], gn[:, hc]))
    o_ref[...] = jnp.concatenate(outs, axis=1)


def _hgrn_sample(proj, lb_logits, g_norm, state, nb, t, nh=4):
    gb = _TILE // t
    nm, mk = _hgrn_consts(t)
    h = A_HEADS
    hg = h // nh
    sel = np.zeros((_TILE, gb * A_DV), np.float32)
    for j in range(gb):
        sel[j * t:(j + 1) * t, j * A_DV:(j + 1) * A_DV] = 1.0
    sel = jnp.asarray(sel)

    def col(off):
        return pl.BlockSpec((_TILE, nh * A_DK), lambda i, hi: (i, off + hi))

    body = functools.partial(_hgrn_s_body, chunk_len=t, gb=gb, nh=nh)
    st_spec = pl.BlockSpec((1, gb, nh, A_DK, A_DV), lambda i, hi: (0, i, hi, 0, 0))
    return pl.pallas_call(
        body,
        grid=(nb // gb, hg),
        in_specs=[col(0), col(hg), col(2 * hg), col(3 * hg),
                  pl.BlockSpec((lb_logits.shape[0], nh * A_DK), lambda i, hi: (0, hi)),
                  pl.BlockSpec((1, nh * A_DV), lambda i, hi: (0, hi)),
                  pl.BlockSpec(nm.shape, lambda i, hi: (0, 0)),
                  pl.BlockSpec(mk.shape, lambda i, hi: (0, 0)),
                  pl.BlockSpec(sel.shape, lambda i, hi: (0, 0)),
                  st_spec],
        out_specs=[pl.BlockSpec((_TILE, nh * A_DV), lambda i, hi: (i, hi)), st_spec],
        out_shape=[jax.ShapeDtypeStruct((nb * t, h * A_DV), BF),
                   jax.ShapeDtypeStruct(state.shape, F32)],
        compiler_params=_params(("arbitrary", "arbitrary")),
        name="hgrn_sample",
    )(proj, proj, proj, proj, lb_logits, g_norm, nm, mk, sel, state)


def _mla_prep_body(p_ref, cos_ref, sin_ref, gq_ref, gkv_ref, wq_ref, wuk_ref,
                   q_ref, kv_ref, ckv_ref, kpe_ref):
    p = p_ref[...]
    cos = cos_ref[...]
    sin = sin_ref[...]
    qn = _rms(p[:, :Q_RANK], gq_ref[...]).astype(BF)
    qf = _dot(qn, wq_ref[...])
    hw = B_HEADS * NOPE_DIM
    for h in range(B_HEADS):
        c0 = h * NOPE_DIM
        q_lat = _dot(qf[:, c0:c0 + NOPE_DIM].astype(BF), wuk_ref[h])
        q_pe = qf[:, hw + c0:hw + c0 + LANES] * cos + qf[:, 2 * hw + c0:2 * hw + c0 + LANES] * sin
        q_ref[:, h * QK_DIM:h * QK_DIM + KV_RANK] = q_lat.astype(q_ref.dtype)
        q_ref[:, h * QK_DIM + KV_RANK:(h + 1) * QK_DIM] = q_pe.astype(q_ref.dtype)
    ckv = _rms(p[:, Q_RANK:Q_RANK + KV_RANK], gkv_ref[...])
    o = Q_RANK + KV_RANK
    kpe = p[:, o:o + LANES] * cos + p[:, o + LANES:o + 2 * LANES] * sin
    ckv_ref[...] = ckv
    kpe_ref[...] = kpe[:, :ROPE_DIM]
    kv_ref[:, :KV_RANK] = ckv.astype(kv_ref.dtype)
    kv_ref[:, KV_RANK:] = kpe.astype(kv_ref.dtype)


def _mla_prep(proj_mla, cos, sin, g_q, g_kv, wq, wuk, tm, n_pos_tiles, act_dtype):
    rows, pc = proj_mla.shape
    return pl.pallas_call(
        _mla_prep_body,
        grid=(rows // tm,),
        in_specs=[pl.BlockSpec((tm, pc), lambda i: (i, 0)),
                  pl.BlockSpec((tm, LANES), lambda i: (i % n_pos_tiles, 0)),
                  pl.BlockSpec((tm, LANES), lambda i: (i % n_pos_tiles, 0)),
                  pl.BlockSpec((1, Q_RANK), lambda i: (0, 0)),
                  pl.BlockSpec((1, KV_RANK), lambda i: (0, 0)),
                  pl.BlockSpec(wq.shape, lambda i: (0, 0)),
                  pl.BlockSpec(wuk.shape, lambda i: (0, 0, 0))],
        out_specs=[pl.BlockSpec((tm, B_HEADS * QK_DIM), lambda i: (i, 0)),
                   pl.BlockSpec((tm, QK_DIM), lambda i: (i, 0)),
                   pl.BlockSpec((tm, KV_RANK), lambda i: (i, 0)),
                   pl.BlockSpec((tm, ROPE_DIM), lambda i: (i, 0))],
        out_shape=[jax.ShapeDtypeStruct((rows, B_HEADS * QK_DIM), act_dtype),
                   jax.ShapeDtypeStruct((rows, QK_DIM), act_dtype),
                   jax.ShapeDtypeStruct((rows, KV_RANK), F32),
                   jax.ShapeDtypeStruct((rows, ROPE_DIM), F32)],
        compiler_params=_params(("arbitrary",)),
        name="mla_prep",
    )(proj_mla, cos, sin, g_q, g_kv, wq, wuk)


_EXP2_SCALE = MLA_SCALE * math.log2(math.e)


def _lane_blocks(x):
    return [x[:, k * LANES:(k + 1) * LANES] for k in range(x.shape[1] // LANES)]


def _softmax_step(s, v, m_ref, l_ref, acc_ref, rows=slice(None)):
    blocks = _lane_blocks(s)
    m_prev = m_ref[rows, :]
    m_new = jnp.maximum(m_prev, jnp.max(functools.reduce(jnp.maximum, blocks), axis=-1, keepdims=True))
    m_ref[rows, :] = m_new
    alpha = jnp.exp2((m_prev - m_new) * _EXP2_SCALE)
    ps = [jnp.exp2((blk - m_new) * _EXP2_SCALE) for blk in blocks]
    l_ref[rows, :] = alpha * l_ref[rows, :] + functools.reduce(jnp.add, ps)
    pv = _dot(jnp.concatenate(ps, axis=1).astype(BF), v)
    for k, blk in enumerate(_lane_blocks(pv)):
        cols = slice(k * LANES, (k + 1) * LANES)
        acc_ref[rows, cols] = alpha * acc_ref[rows, cols] + blk


def _softmax_finish(acc, l_part):
    return acc / jnp.sum(l_part, axis=-1, keepdims=True)


def _attn_p_body(q_ref, kv_ref, wuv_ref, o_ref, *scratch, tq, hpar):
    qi = pl.program_id(1)
    half = tq // 2
    stats = [scratch[3 * i:3 * i + 3] for i in range(hpar)]
    for h0 in range(0, B_HEADS, hpar):
        qs = [q_ref[0, :, (h0 + i) * QK_DIM:(h0 + i + 1) * QK_DIM] for i in range(hpar)]
        for acc_ref, m_ref, l_ref in stats:
            m_ref[...] = jnp.full_like(m_ref, -jnp.inf)
            l_ref[...] = jnp.zeros_like(l_ref)
            acc_ref[...] = jnp.zeros_like(acc_ref)

        def body(j, carry):
            kv = kv_ref[0, pl.ds(pl.multiple_of(j * tq, tq), tq), :]
            ss = [_dot_nt(qh, kv) for qh in qs]
            for s, (acc_ref, m_ref, l_ref) in zip(ss, stats):
                _softmax_step(s, kv[:, :KV_RANK], m_ref, l_ref, acc_ref)
            return carry

        lax.fori_loop(0, qi, body, 0)

        kv = kv_ref[0, pl.ds(pl.multiple_of(qi * tq, tq), tq), :]
        parts = [(slice(0, half), half), (slice(half, tq), tq)]
        ss = [[_dot_nt(qh[r], kv[:nk]) for r, nk in parts] for qh in qs]
        for s_parts, (acc_ref, m_ref, l_ref) in zip(ss, stats):
            for (r, nk), s in zip(parts, s_parts):
                qpos = lax.broadcasted_iota(jnp.int32, s.shape, 0) + r.start
                kpos = lax.broadcasted_iota(jnp.int32, s.shape, 1)
                s = jnp.where(kpos <= qpos, s, -jnp.inf)
                _softmax_step(s, kv[:nk, :KV_RANK], m_ref, l_ref, acc_ref, rows=r)
        for i, (acc_ref, m_ref, l_ref) in enumerate(stats):
            h = h0 + i
            o_lat = _softmax_finish(acc_ref[...], l_ref[...]).astype(BF)
            o_ref[0, :, h * V_DIM:(h + 1) * V_DIM] = _dot(o_lat, wuv_ref[h]).astype(o_ref.dtype)


def _attn_prompt(q_cat, kv_cat, wuv, b, t, tq=512, hpar=4):
    q3 = q_cat.reshape(b, t, q_cat.shape[-1])
    kv3 = kv_cat.reshape(b, t, kv_cat.shape[-1])
    body = functools.partial(_attn_p_body, tq=tq, hpar=hpar)
    return pl.pallas_call(
        body,
        grid=(b, t // tq),
        in_specs=[pl.BlockSpec((1, tq, q3.shape[-1]), lambda bi, qi: (bi, qi, 0)),
                  pl.BlockSpec((1, t, QK_DIM), lambda bi, qi: (bi, 0, 0)),
                  pl.BlockSpec(wuv.shape, lambda bi, qi: (0, 0, 0))],
        out_specs=pl.BlockSpec((1, tq, B_HEADS * V_DIM), lambda bi, qi: (bi, qi, 0)),
        out_shape=jax.ShapeDtypeStruct((b, t, B_HEADS * V_DIM), BF),
        scratch_shapes=[pltpu.VMEM((tq, KV_RANK), F32),
                        pltpu.VMEM((tq, LANES), F32),
                        pltpu.VMEM((tq, LANES), F32)] * hpar,
        compiler_params=_params(("arbitrary", "arbitrary")),
        name="attn_prompt",
    )(q3, kv3, wuv)


def _attn_s_body(pt_ref, q_ref, kvn_ref, ckv_hbm, kpe_hbm, wuv_ref, o_ref,
                 ckv_buf, kpe_buf, sem, qs_ref, acc_ref, m_ref, l_ref,
                 *, nch, ppc, n_pages, t_new, nslots):
    b = pl.program_id(0)
    ahead = nslots

    def page_copies(bi, c, p):
        slot = c % nslots
        page = pt_ref[bi * n_pages + c * ppc + p]
        span = pl.ds(p * PAGE_SIZE, PAGE_SIZE)
        return (pltpu.make_async_copy(ckv_hbm.at[page], ckv_buf.at[slot, span, :], sem.at[slot, 0]),
                pltpu.make_async_copy(kpe_hbm.at[page], kpe_buf.at[slot, p], sem.at[slot, 1]))

    def start_all(bi, c):
        for p in range(ppc):
            for cp in page_copies(bi, c, p):
                cp.start()

    def wait_all(bi, c):
        for p in range(ppc):
            for cp in page_copies(bi, c, p):
                cp.wait()

    @pl.when(b == 0)
    def _():
        for c in range(ahead):
            start_all(b, c)

    for h in range(B_HEADS):
        qs_ref[h * t_new:(h + 1) * t_new, :] = q_ref[0, :, h * QK_DIM:(h + 1) * QK_DIM]
    m_ref[...] = jnp.full_like(m_ref, -jnp.inf)
    l_ref[...] = jnp.zeros_like(l_ref)
    acc_ref[...] = jnp.zeros_like(acc_ref)
    qs = qs_ref[...]
    qb = qs.astype(BF)
    q_lat = qb[:, :KV_RANK]
    q_pe = qb[:, KV_RANK:KV_RANK + ROPE_DIM]

    def scores(c):
        wait_all(b, c)
        slot = c % nslots
        kc = ckv_buf[slot].astype(BF)
        kpt = jnp.concatenate([kpe_buf[slot, p] for p in range(ppc)], axis=1).astype(BF)
        return kc, _dot_nt(q_lat, kc) + _dot(q_pe, kpt)

    cur = scores(0)
    for c in range(nch):
        nxt = c + ahead
        if nxt < nch:
            start_all(b, nxt)
        else:
            @pl.when(b + 1 < pl.num_programs(0))
            def _():
                start_all(b + 1, nxt - nch)
        following = scores(c + 1) if c + 1 < nch else None
        _softmax_step(cur[1], cur[0], m_ref, l_ref, acc_ref)
        cur = following

    kn = kvn_ref[0]
    rows = B_HEADS * t_new
    sn = _dot_nt(qs, kn)
    tq_ = lax.broadcasted_iota(jnp.int32, (rows, t_new), 0) % t_new
    tk_ = lax.broadcasted_iota(jnp.int32, (rows, t_new), 1)
    sn = jnp.where(tk_ <= tq_, sn, -jnp.inf)
    m_prev = m_ref[:, 0:1]
    m_new = jnp.maximum(m_prev, jnp.max(sn, axis=-1, keepdims=True))
    alpha = jnp.exp2((m_prev - m_new) * _EXP2_SCALE)
    pn = jnp.exp2((sn - m_new) * _EXP2_SCALE)
    l = alpha * jnp.sum(l_ref[...], axis=-1, keepdims=True) + jnp.sum(pn, axis=-1, keepdims=True)
    acc = alpha * acc_ref[...]
    vn = kn[:, :KV_RANK].astype(BF).astype(F32)
    pn = pn.astype(BF).astype(F32)
    for j in range(t_new):
        acc = acc + pn[:, j:j + 1] * vn[j:j + 1, :]
    o_lat = (acc / l).astype(BF)
    r = _dot(o_lat, wuv_ref[...])
    for h in range(B_HEADS):
        o_ref[0, :, h * V_DIM:(h + 1) * V_DIM] = r[h * t_new:(h + 1) * t_new, h * V_DIM:(h + 1) * V_DIM]


def _attn_sample(page_table, q_cat, kv_cat, cache_ckv, cache_kpe, wuv_all, nb, t_new, ppc=16, nslots=4):
    n_pages = page_table.shape[1]
    nch = n_pages // ppc
    assert nch % nslots == 0
    keys = ppc * PAGE_SIZE
    q3 = q_cat.reshape(nb, t_new, q_cat.shape[-1])
    kv3 = kv_cat.reshape(nb, t_new, kv_cat.shape[-1])
    rows = B_HEADS * t_new
    body = functools.partial(_attn_s_body, nch=nch, ppc=ppc, n_pages=n_pages, t_new=t_new, nslots=nslots)
    grid_spec = pltpu.PrefetchScalarGridSpec(
        num_scalar_prefetch=1,
        grid=(nb,),
        in_specs=[pl.BlockSpec((1, t_new, q3.shape[-1]), lambda bi, pt: (bi, 0, 0)),
                  pl.BlockSpec((1, t_new, QK_DIM), lambda bi, pt: (bi, 0, 0)),
                  pl.BlockSpec(memory_space=pl.ANY),
                  pl.BlockSpec(memory_space=pl.ANY),
                  pl.BlockSpec(wuv_all.shape, lambda bi, pt: (0, 0))],
        out_specs=pl.BlockSpec((1, t_new, B_HEADS * V_DIM), lambda bi, pt: (bi, 0, 0)),
        scratch_shapes=[pltpu.VMEM((nslots, keys, KV_RANK), F32),
                        pltpu.VMEM((nslots, ppc, ROPE_DIM, PAGE_SIZE), F32),
                        pltpu.SemaphoreType.DMA((nslots, 2)),
                        pltpu.VMEM((rows, QK_DIM), F32),
                        pltpu.VMEM((rows, KV_RANK), F32),
                        pltpu.VMEM((rows, LANES), F32),
                        pltpu.VMEM((rows, LANES), F32)])
    return pl.pallas_call(
        body,
        grid_spec=grid_spec,
        out_shape=jax.ShapeDtypeStruct((nb, t_new, B_HEADS * V_DIM), F32),
        compiler_params=_params(("arbitrary",)),
        name="attn_sample",
    )(page_table.reshape(-1), q3, kv3, cache_ckv, cache_kpe, wuv_all)


def _mix_body(oa_ref, ob_ref, ga_ref, gb_ref, x_ref, gt_ref, g_ref, wa_ref, wb_ref, wo_ref, o_ref):
    ya = _dot(oa_ref[...].astype(BF), wa_ref[...])
    yb = _dot(ob_ref[...].astype(BF), wb_ref[...])
    merged = jax.nn.sigmoid(ga_ref[...]) * ya + jax.nn.sigmoid(gb_ref[...]) * yb
    z = _dot(merged.astype(BF), wo_ref[...])
    x = x_ref[...]
    o_ref[...] = x + gt_ref[...] * _rms(z, g_ref[...]).reshape(x.shape)


def _mix(o_a, o_b, proj, x, mod, g, wa, wb, wo, gb, tr):
    bm, r, d = x.shape
    tm = gb * tr
    nr = r // tr
    gate0 = 4 * A_HEADS * A_DK // d

    def rows2(width, cb):
        return pl.BlockSpec((tm, width), lambda a, i: (a * nr + i, cb))

    def const(w):
        return pl.BlockSpec(w.shape, lambda a, i: (0, 0))

    return pl.pallas_call(
        _mix_body,
        grid=(bm // gb, nr),
        in_specs=[rows2(o_a.shape[1], 0), rows2(o_b.shape[1], 0),
                  rows2(d, gate0), rows2(d, gate0 + 1),
                  pl.BlockSpec((gb, tr, d), lambda a, i: (a, i, 0)),
                  pl.BlockSpec((gb, 1, d), lambda a, i: (a, 0, 2)),
                  const(g), const(wa), const(wb), const(wo)],
        out_specs=pl.BlockSpec((gb, tr, d), lambda a, i: (a, i, 0)),
        out_shape=jax.ShapeDtypeStruct(x.shape, F32),
        compiler_params=_params(("arbitrary", "arbitrary")),
        name="mix",
    )(o_a, o_b, proj, proj, x, mod, g, wa, wb, wo)


def _mlp_body(x_ref, sh_ref, sc_ref, gt_ref, g1_ref, g2_ref, wu_ref, wd_ref, o_ref, h_ref, acc_ref):
    f = pl.program_id(2)

    @pl.when(f == 0)
    def _():
        x = x_ref[...]
        h = _rms(x, g1_ref[...]) * (1.0 + sc_ref[...]) + sh_ref[...]
        h_ref[...] = h.reshape(h_ref.shape).astype(BF)
        acc_ref[...] = jnp.zeros_like(acc_ref)

    u = jnp.maximum(_dot(h_ref[...], wu_ref[...]), 0.0)
    acc_ref[...] += _dot((u * u).astype(BF), wd_ref[...])

    @pl.when(f == pl.num_programs(2) - 1)
    def _():
        x = x_ref[...]
        o_ref[...] = x + gt_ref[...] * _rms(acc_ref[...], g2_ref[...]).reshape(x.shape)


def _mlp(x, mod, g1, g2, wu, wd, gb, tr, tf=1024):
    bm, r, d = x.shape
    tm = gb * tr
    nr = r // tr
    ff = wu.shape[1]

    def modspec(piece):
        return pl.BlockSpec((gb, 1, d), lambda a, i, f: (a, 0, piece))

    return pl.pallas_call(
        _mlp_body,
        grid=(bm // gb, nr, ff // tf),
        in_specs=[pl.BlockSpec((gb, tr, d), lambda a, i, f: (a, i, 0)),
                  modspec(3), modspec(4), modspec(5),
                  pl.BlockSpec((1, d), lambda a, i, f: (0, 0)),
                  pl.BlockSpec((1, d), lambda a, i, f: (0, 0)),
                  pl.BlockSpec((d, tf), lambda a, i, f: (0, f)),
                  pl.BlockSpec((tf, d), lambda a, i, f: (f, 0))],
        out_specs=pl.BlockSpec((gb, tr, d), lambda a, i, f: (a, i, 0)),
        out_shape=jax.ShapeDtypeStruct(x.shape, F32),
        scratch_shapes=[pltpu.VMEM((tm, d), BF), pltpu.VMEM((tm, d), F32)],
        compiler_params=_params(("arbitrary", "arbitrary", "arbitrary")),
        name="mlp",
    )(x, mod, mod, mod, g1, g2, wu, wd)


def _rope_tables(pos):
    half = ROPE_DIM // 2
    inv = ROPE_THETA ** (-jnp.arange(half, dtype=F32) / half)
    ang = pos.astype(F32)[:, None] * inv[None, :]
    z = jnp.zeros((pos.shape[0], LANES - ROPE_DIM), F32)
    cos = jnp.cos(ang)
    sin = jnp.sin(ang)
    return jnp.concatenate([cos, cos, z], axis=1), jnp.concatenate([sin, sin, z], axis=1)


def _swap_halves(w):
    half = w.shape[-1] // 2
    return jnp.concatenate([-w[..., half:], w[..., :half]], axis=-1)


def _pad_lanes(w):
    pad = [(0, 0)] * (w.ndim - 1) + [(0, LANES - w.shape[-1])]
    return jnp.pad(w, pad)


def kernel(x_prompt, x_sample, c_prompt, c_sample, cache_ckv, cache_kpe, state_hgrn, page_table, w_ada, b_ada, g_pre_mix, g_post_mix, g_pre_mlp, g_post_mlp, w_in, lb_logits, g_hgrn_norm, w_a_out, g_q_norm, w_q_up, g_kv_norm, w_kv_up, w_b_out, w_o, w_up, w_down):
    depth = w_in.shape[0]
    assert depth == 1
    batch, seq, d = x_prompt.shape
    nb, t_new, _ = x_sample.shape
    past_len = page_table.shape[1] * PAGE_SIZE
    hk = A_HEADS * A_DK

    wit = w_in[0].T
    o_qd = 3 * hk + A_HEADS * A_DV
    o_kpe = o_qd + Q_RANK + KV_RANK
    o_gate = o_kpe + ROPE_DIM
    wit = wit.astype(BF)
    w_kpe = wit[o_kpe:o_gate]
    half = ROPE_DIM // 2
    w_kpe_swapped = jnp.concatenate([-w_kpe[half:], w_kpe[:half]], axis=0)
    zpad = jnp.zeros((LANES - ROPE_DIM, d), BF)
    w_mla = jnp.concatenate([wit[o_qd:o_kpe], w_kpe, zpad, w_kpe_swapped, zpad], axis=0)
    wq = w_q_up[0].reshape(Q_RANK, B_HEADS, NOPE_DIM + ROPE_DIM)
    wq_pe = wq[..., NOPE_DIM:]
    wq_cat = jnp.concatenate([wq[..., :NOPE_DIM].reshape(Q_RANK, -1),
                              _pad_lanes(wq_pe).reshape(Q_RANK, -1),
                              _pad_lanes(_swap_halves(wq_pe)).reshape(Q_RANK, -1)], axis=1).astype(BF)
    wkv = w_kv_up[0].reshape(KV_RANK, B_HEADS, NOPE_DIM + V_DIM)
    wuk = wkv[..., :NOPE_DIM].transpose(1, 2, 0).astype(BF)
    wuv = wkv[..., NOPE_DIM:].transpose(1, 0, 2).astype(BF)
    wuv_all = wkv[..., NOPE_DIM:].reshape(KV_RANK, B_HEADS * V_DIM).astype(BF)
    wa = w_a_out[0].astype(BF)
    wb = w_b_out[0].astype(BF)
    wo = w_o[0].astype(BF)
    wu = w_up[0].astype(BF)
    wd = w_down[0].astype(BF)

    c_all = jnp.concatenate([c_prompt, c_sample], axis=0)
    mod = _ada(c_all, w_ada[0], b_ada[0][None, :])
    mod = mod.reshape(batch + nb, 1, 6 * d)
    mod_p, mod_s = mod[:batch], mod[batch:]

    cos_p, sin_p = _rope_tables(jnp.arange(seq))
    cos_s, sin_s = _rope_tables(past_len + jnp.arange(t_new))
    gs = _TILE
    reps = gs
    cos_s = jnp.tile(cos_s, (reps, 1))
    sin_s = jnp.tile(sin_s, (reps, 1))

    def layer(x, mod_g, gb, tr, hgrn, attend, cos, sin, n_pos_tiles, act_dtype):
        tm = gb * tr
        proj, proj_mla = _proj(x, mod_g, g_pre_mix, wit, (o_qd, o_gate - o_qd), w_mla, gb, tr, 2048)
        o_a, s_fin = hgrn(proj)
        q_cat, kv_cat, ckv, kpe = _mla_prep(proj_mla, cos, sin, g_q_norm, g_kv_norm, wq_cat, wuk,
                                            tm, n_pos_tiles, act_dtype)
        o_b = attend(q_cat, kv_cat)
        mix_gb, mix_tr = (1, tr // 2) if gb == 1 else (gb // 2, tr)
        x1 = _mix(o_a.reshape(-1, o_a.shape[-1]), o_b.reshape(-1, o_b.shape[-1]), proj, x, mod_g,
                  g_post_mix, wa, wb, wo, mix_gb, mix_tr)
        y = _mlp(x1, mod_g, g_pre_mlp, g_post_mlp, wu, wd, gb, tr)
        return y, ckv, kpe, s_fin

    kpe_pages_t = jnp.swapaxes(cache_kpe[0], 1, 2)

    tr_p = 512
    y_p, ckv_p, kpe_p, s_p = layer(
        x_prompt, mod_p, 1, tr_p,
        lambda proj: _hgrn_prompt(proj, lb_logits, g_hgrn_norm, batch, seq),
        lambda q, kv: _attn_prompt(q, kv, wuv, batch, seq),
        cos_p, sin_p, seq // tr_p, BF)
    y_s, ckv_s, kpe_s, s_s = layer(
        x_sample, mod_s, gs, t_new,
        lambda proj: _hgrn_sample(proj, lb_logits, g_hgrn_norm, state_hgrn, nb, t_new),
        lambda q, kv: _attn_sample(page_table, q, kv, cache_ckv[0], kpe_pages_t, wuv_all, nb, t_new),
        cos_s, sin_s, 1, F32)

    return (y_p, y_s,
            ckv_p.reshape(1, batch, seq, KV_RANK), kpe_p.reshape(1, batch, seq, ROPE_DIM),
            s_p[None],
            ckv_s.reshape(1, nb, t_new, KV_RANK), kpe_s.reshape(1, nb, t_new, ROPE_DIM),
            s_s)
```

```python
import functools
import math

import numpy as np
import jax
import jax.numpy as jnp
from jax import lax
from jax.experimental import pallas as pl
from jax.experimental.pallas import tpu as pltpu

BF = jnp.bfloat16
F32 = jnp.float32

D_MODEL = 2048
A_HEADS = 8
A_DK = 128
A_DV = 128
A_CHUNK = 64
B_HEADS = 8
Q_RANK = 512
KV_RANK = 512
NOPE_DIM = 128
ROPE_DIM = 64
V_DIM = 128
ROPE_THETA = 10000.0
MLA_SCALE = (NOPE_DIM + ROPE_DIM) ** -0.5
D_FF = 4 * D_MODEL
EPS = 1e-6
PAGE_SIZE = 128

LANES = 128
QK_DIM = KV_RANK + LANES
VMEM_LIMIT = 56 * 1024 * 1024

_NT = (((1,), (1,)), ((), ()))
_TN = (((0,), (0,)), ((), ()))


def _dot(a, b):
    return jnp.dot(a, b, preferred_element_type=F32)


def _dot_nt(a, b):
    return lax.dot_general(a, b, _NT, preferred_element_type=F32)


def _dot_tn(a, b):
    return lax.dot_general(a, b, _TN, preferred_element_type=F32)


def _rms(x, g):
    r = lax.rsqrt(jnp.mean(x * x, axis=-1, keepdims=True) + EPS)
    return (x * r) * g


def _params(sem):
    return pltpu.CompilerParams(dimension_semantics=sem, vmem_limit_bytes=VMEM_LIMIT)


def _ada_body(c_ref, w_ref, b_ref, o_ref):
    c = c_ref[...]
    s = (c * jax.nn.sigmoid(c)).astype(BF)
    o_ref[...] = _dot(s, w_ref[...].astype(BF)) + b_ref[...]


def _ada(c_all, w_ada, b_ada, tn=1024):
    m, d = c_all.shape
    n = w_ada.shape[1]
    return pl.pallas_call(
        _ada_body,
        grid=(n // tn,),
        in_specs=[pl.BlockSpec((m, d), lambda j: (0, 0)),
                  pl.BlockSpec((d, tn), lambda j: (0, j)),
                  pl.BlockSpec((1, tn), lambda j: (0, j))],
        out_specs=pl.BlockSpec((m, tn), lambda j: (0, j)),
        out_shape=jax.ShapeDtypeStruct((m, n), F32),
        compiler_params=_params(("arbitrary",)),
        name="ada",
    )(c_all, w_ada, b_ada)


def _proj_body(x_ref, sh_ref, sc_ref, g_ref, w_ref, o_ref, h_ref):
    @pl.when(pl.program_id(2) == 0)
    def _():
        x = x_ref[...]
        h = _rms(x, g_ref[...]) * (1.0 + sc_ref[...]) + sh_ref[...]
        h_ref[...] = h.reshape(h_ref.shape).astype(BF)

    o_ref[...] = _dot_nt(h_ref[...], w_ref[...])


def _proj(x, mod, g, w_t, gb, tr, tn, skip=None):
    bm, r, d = x.shape
    tm = gb * tr
    nr = r // tr
    if skip is None:
        n = w_t.shape[0]
        w_spec = pl.BlockSpec((tn, d), lambda a, i, j: (j, 0))
    else:
        first, gap = skip[0] // tn, skip[1]
        n = w_t.shape[0] - gap
        w_spec = pl.BlockSpec((pl.Element(tn), pl.Element(d)),
                              lambda a, i, j: (pl.multiple_of(j * tn + jnp.where(j >= first, gap, 0),
                                                              math.gcd(tn, gap)), 0))
    return pl.pallas_call(
        _proj_body,
        grid=(bm // gb, nr, n // tn),
        in_specs=[pl.BlockSpec((gb, tr, d), lambda a, i, j: (a, i, 0)),
                  pl.BlockSpec((gb, 1, d), lambda a, i, j: (a, 0, 0)),
                  pl.BlockSpec((gb, 1, d), lambda a, i, j: (a, 0, 1)),
                  pl.BlockSpec((1, d), lambda a, i, j: (0, 0)),
                  w_spec],
        out_specs=pl.BlockSpec((tm, tn), lambda a, i, j: (a * nr + i, j)),
        out_shape=jax.ShapeDtypeStruct((bm * r, n), F32),
        scratch_shapes=[pltpu.VMEM((tm, d), BF)],
        compiler_params=_params(("arbitrary", "arbitrary", "arbitrary")),
        name="proj",
    )(x, mod, mod, g, w_t)


_TILE = 64


def _hgrn_consts(chunk):
    t = np.arange(_TILE)[:, None]
    u = np.arange(_TILE)[None, :]
    tril = ((t // chunk) == (u // chunk)) & (u <= t)
    masks = []
    m = chunk
    while m >= 2:
        h = m // 2
        masks.append(((t // m) == (u // m)) & ((t % m) >= h) & ((u % m) < h))
        m = h
    masks.append(t == u)
    mk = np.concatenate([a.astype(np.float32) for a in masks], axis=0)
    return jnp.asarray(tril.astype(np.float32), BF), jnp.asarray(mk, F32)


def _split3(x):
    hi = x.astype(BF)
    r1 = x - hi.astype(F32)
    mid = r1.astype(BF)
    lo = (r1 - mid.astype(F32)).astype(BF)
    return hi, mid, lo


def _hgrn_gates(qa, fa, lb):
    q = qa * jax.nn.sigmoid(qa)
    f = lb + (1.0 - lb) * jax.nn.sigmoid(fa)
    return q, jnp.log(f), 1.0 - f


def _hgrn_lb(lbl):
    e = jnp.exp(lbl - jnp.max(lbl, axis=0, keepdims=True))
    return e[0:1] / jnp.sum(e, axis=0, keepdims=True)


def _hgrn_scores(q, k, levels, mk_ref):
    nlev = len(levels)
    sc = jnp.where(mk_ref[nlev * _TILE:(nlev + 1) * _TILE, :] > 0.5,
                   _dot_nt(q.astype(BF), k.astype(BF)), 0.0)
    row = lax.broadcasted_iota(jnp.int32, q.shape, 0)
    for l, e in enumerate(levels):
        m = (1 << nlev) >> l
        x = (jnp.where((row & (m // 2)) != 0, q, k) * e).astype(BF)
        p = _dot_nt(x, x)
        sc = sc + jnp.where(mk_ref[l * _TILE:(l + 1) * _TILE, :] > 0.5, p, 0.0)
    return sc


def _row_bcast(x, rows, n):
    w = x.shape[1]
    return jnp.concatenate([jnp.broadcast_to(x[r:r + 1, :], (n, w)) for r in rows], axis=0)


def _hgrn_decays(tril_ref, logf, chunk):
    w = logf.shape[1]
    a = _dot(tril_ref[...], jnp.concatenate(_split3(logf), axis=1))
    b = (a[:, :w] + a[:, w:2 * w] + a[:, 2 * w:]) * math.log2(math.e)
    sub = lax.broadcasted_iota(jnp.int32, (8, w), 0)

    def refs_in_groups(offs):
        span = 8 // len(offs)
        pieces = []
        for g in range(_TILE // 8):
            rows = [jnp.broadcast_to(b[g * 8 + o:g * 8 + o + 1, :], (8, w)) for o in offs]
            ref = rows[-1]
            for i in range(len(offs) - 2, -1, -1):
                ref = jnp.where(sub < (i + 1) * span, rows[i], ref)
            pieces.append(ref)
        return jnp.concatenate(pieces, axis=0)

    b_last = _row_bcast(b, range(chunk - 1, _TILE, chunk), chunk)
    levels = []
    m = chunk
    while m >= 2:
        if m >= 16:
            ref = _row_bcast(b, range(m // 2, _TILE, m), m)
        else:
            ref = refs_in_groups(list(range(m // 2, 8, m)))
        levels.append(jnp.exp2(-jnp.abs(b - ref)))
        m //= 2
    return jnp.exp2(b), jnp.exp2(b_last - b), levels


def _hgrn_out(o, ga, gn):
    return (_rms(o, gn) * jax.nn.sigmoid(ga)).astype(BF)


def _hgrn_p_body(qa_ref, fa_ref, ia_ref, ga_ref, lb_ref, gn_ref, nm_ref, mk_ref,
                 o_ref, s_ref, st_ref, *, nchunks, chunk_len, nh):
    ti = pl.program_id(2)

    @pl.when(ti == 0)
    def _():
        st_ref[...] = jnp.zeros_like(st_ref)

    lb = _hgrn_lb(lb_ref[...])
    gn = gn_ref[...]

    def chunk(c, carry):
        rows = pl.ds(pl.multiple_of(c * _TILE, _TILE), _TILE)
        qa, fa, ia, ga = (r[0, rows, :] for r in (qa_ref, fa_ref, ia_ref, ga_ref))
        sts = [st_ref[j] for j in range(nh)]
        hs = range(nh)
        cols = [slice(j * A_DK, (j + 1) * A_DK) for j in hs]
        q, logf, k = _hgrn_gates(qa, fa, lb)
        eb, er, levels = _hgrn_decays(nm_ref, logf, chunk_len)
        qe = (q * eb).astype(BF)
        ke = (k * er).astype(BF)
        vs = [ia[:, c].astype(BF) for c in cols]
        o_inter = [_dot_nt(qe[:, c], sts[j].astype(BF)) for j, c in enumerate(cols)]
        scs = [_hgrn_scores(q[:, c], k[:, c], [e[:, c] for e in levels], mk_ref) for c in cols]
        os_ = [o_inter[j] + _dot(scs[j].astype(BF), vs[j]) for j in hs]
        upd = [_dot_tn(vs[j], ke[:, c]) for j, c in enumerate(cols)]
        for j, c in enumerate(cols):
            st_ref[j] = sts[j] * eb[_TILE - 1:_TILE, c] + upd[j]
        o_ref[0, rows, :] = jnp.concatenate(
            [_hgrn_out(os_[j], ga[:, c], gn[:, c]) for j, c in enumerate(cols)], axis=1)
        return carry

    lax.fori_loop(0, nchunks, chunk, 0)

    @pl.when(ti == pl.num_programs(2) - 1)
    def _():
        for j in range(nh):
            s_ref[0, j] = st_ref[j].T


def _hgrn_prompt(proj, lb_logits, g_norm, b, t, tt=512, nh=8):
    nm, mk = _hgrn_consts(A_CHUNK)
    h = A_HEADS
    hg = h // nh
    proj3 = proj.reshape(b, t, proj.shape[-1])

    def col(off):
        return pl.BlockSpec((1, tt, nh * A_DK), lambda bi, hi, ti: (bi, ti, off + hi))

    body = functools.partial(_hgrn_p_body, nchunks=tt // _TILE, chunk_len=A_CHUNK, nh=nh)
    return pl.pallas_call(
        body,
        grid=(b, hg, t // tt),
        in_specs=[col(0), col(hg), col(2 * hg), col(3 * hg),
                  pl.BlockSpec((lb_logits.shape[0], nh * A_DK), lambda bi, hi, ti: (0, hi)),
                  pl.BlockSpec((1, nh * A_DV), lambda bi, hi, ti: (0, hi)),
                  pl.BlockSpec(nm.shape, lambda bi, hi, ti: (0, 0)),
                  pl.BlockSpec(mk.shape, lambda bi, hi, ti: (0, 0))],
        out_specs=[pl.BlockSpec((1, tt, nh * A_DV), lambda bi, hi, ti: (bi, ti, hi)),
                   pl.BlockSpec((1, nh, A_DK, A_DV), lambda bi, hi, ti: (bi, hi, 0, 0))],
        out_shape=[jax.ShapeDtypeStruct((b, t, h * A_DV), BF),
                   jax.ShapeDtypeStruct((b, h, A_DK, A_DV), F32)],
        scratch_shapes=[pltpu.VMEM((nh, A_DV, A_DK), F32)],
        compiler_params=_params(("arbitrary", "arbitrary", "arbitrary")),
        name="hgrn_prompt",
    )(proj3, proj3, proj3, proj3, lb_logits, g_norm, nm, mk)


def _hgrn_s_body(qa_ref, fa_ref, ia_ref, ga_ref, lb_ref, gn_ref, nm_ref, mk_ref, sel_ref,
                 s0_ref, o_ref, s_ref, *, chunk_len, gb, nh):
    lb = _hgrn_lb(lb_ref[...])
    q_all, logf_all, k_all = _hgrn_gates(qa_ref[...], fa_ref[...], lb)
    eb_all, er_all, levels_all = _hgrn_decays(nm_ref, logf_all, chunk_len)
    sel = sel_ref[...]
    selb = sel.astype(BF)
    gn = gn_ref[...]
    outs = []
    for hh in range(nh):
        hc = slice(hh * A_DK, (hh + 1) * A_DK)
        q, logf, k = q_all[:, hc], logf_all[:, hc], k_all[:, hc]
        vf = ia_ref[:, hc]
        v = vf.astype(BF)
        parts = _split3(logf)
        dec = jnp.exp(_dot_tn(jnp.concatenate(parts, axis=0), jnp.concatenate([selb] * len(parts), axis=0)))
        qe = q * eb_all[:, hc]
        q_blk = (jnp.concatenate([qe] * gb, axis=1) * sel).astype(BF)
        v_blk = (jnp.concatenate([vf] * gb, axis=1) * sel).astype(BF)
        s0 = s0_ref[0, :, hh]
        o = _dot(q_blk, s0.reshape(gb * A_DK, A_DV).astype(BF))
        sc = _hgrn_scores(q, k, [e[:, hc] for e in levels_all], mk_ref)
        o = o + _dot(sc.astype(BF), v)
        upd = _dot_tn((k * er_all[:, hc]).astype(BF), v_blk)
        for j in range(gb):
            cols = slice(j * A_DV, (j + 1) * A_DV)
            s_ref[0, j, hh] = dec[:, cols] * s0[j] + upd[:, cols]
        outs.append(_hgrn_out(o, ga_ref[:, hc], gn[:, hc]))
    o_ref[...] = jnp.concatenate(outs, axis=1)


def _hgrn_sample(proj, lb_logits, g_norm, state, nb, t, nh=4):
    gb = _TILE // t
    nm, mk = _hgrn_consts(t)
    h = A_HEADS
    hg = h // nh
    sel = np.zeros((_TILE, gb * A_DV), np.float32)
    for j in range(gb):
        sel[j * t:(j + 1) * t, j * A_DV:(j + 1) * A_DV] = 1.0
    sel = jnp.asarray(sel)

    def col(off):
        return pl.BlockSpec((_TILE, nh * A_DK), lambda i, hi: (i, off + hi))

    body = functools.partial(_hgrn_s_body, chunk_len=t, gb=gb, nh=nh)
    st_spec = pl.BlockSpec((1, gb, nh, A_DK, A_DV), lambda i, hi: (0, i, hi, 0, 0))
    return pl.pallas_call(
        body,
        grid=(nb // gb, hg),
        in_specs=[col(0), col(hg), col(2 * hg), col(3 * hg),
                  pl.BlockSpec((lb_logits.shape[0], nh * A_DK), lambda i, hi: (0, hi)),
                  pl.BlockSpec((1, nh * A_DV), lambda i, hi: (0, hi)),
                  pl.BlockSpec(nm.shape, lambda i, hi: (0, 0)),
                  pl.BlockSpec(mk.shape, lambda i, hi: (0, 0)),
                  pl.BlockSpec(sel.shape, lambda i, hi: (0, 0)),
                  st_spec],
        out_specs=[pl.BlockSpec((_TILE, nh * A_DV), lambda i, hi: (i, hi)), st_spec],
        out_shape=[jax.ShapeDtypeStruct((nb * t, h * A_DV), BF),
                   jax.ShapeDtypeStruct(state.shape, F32)],
        compiler_params=_params(("arbitrary", "arbitrary")),
        name="hgrn_sample",
    )(proj, proj, proj, proj, lb_logits, g_norm, nm, mk, sel, state)


def _mla_prep_body(p_ref, cos_ref, sin_ref, gq_ref, gkv_ref, wq_ref, wuk_ref,
                   q_ref, kv_ref, ckv_ref, kpe_ref):
    p = p_ref[...]
    cos = cos_ref[...]
    sin = sin_ref[...]
    qn = _rms(p[:, :Q_RANK], gq_ref[...]).astype(BF)
    qf = _dot(qn, wq_ref[...])
    hw = B_HEADS * NOPE_DIM
    for h in range(B_HEADS):
        c0 = h * NOPE_DIM
        q_lat = _dot(qf[:, c0:c0 + NOPE_DIM].astype(BF), wuk_ref[h])
        q_pe = qf[:, hw + c0:hw + c0 + LANES] * cos + qf[:, 2 * hw + c0:2 * hw + c0 + LANES] * sin
        q_ref[:, h * QK_DIM:h * QK_DIM + KV_RANK] = q_lat.astype(q_ref.dtype)
        q_ref[:, h * QK_DIM + KV_RANK:(h + 1) * QK_DIM] = q_pe.astype(q_ref.dtype)
    ckv = _rms(p[:, Q_RANK:Q_RANK + KV_RANK], gkv_ref[...])
    o = Q_RANK + KV_RANK
    kpe = p[:, o:o + LANES] * cos + p[:, o + LANES:o + 2 * LANES] * sin
    ckv_ref[...] = ckv
    kpe_ref[...] = kpe[:, :ROPE_DIM]
    kv_ref[:, :KV_RANK] = ckv.astype(kv_ref.dtype)
    kv_ref[:, KV_RANK:] = kpe.astype(kv_ref.dtype)


def _mla_prep(proj_mla, cos, sin, g_q, g_kv, wq, wuk, tm, n_pos_tiles, act_dtype):
    rows, pc = proj_mla.shape
    return pl.pallas_call(
        _mla_prep_body,
        grid=(rows // tm,),
        in_specs=[pl.BlockSpec((tm, pc), lambda i: (i, 0)),
                  pl.BlockSpec((tm, LANES), lambda i: (i % n_pos_tiles, 0)),
                  pl.BlockSpec((tm, LANES), lambda i: (i % n_pos_tiles, 0)),
                  pl.BlockSpec((1, Q_RANK), lambda i: (0, 0)),
                  pl.BlockSpec((1, KV_RANK), lambda i: (0, 0)),
                  pl.BlockSpec(wq.shape, lambda i: (0, 0)),
                  pl.BlockSpec(wuk.shape, lambda i: (0, 0, 0))],
        out_specs=[pl.BlockSpec((tm, B_HEADS * QK_DIM), lambda i: (i, 0)),
                   pl.BlockSpec((tm, QK_DIM), lambda i: (i, 0)),
                   pl.BlockSpec((tm, KV_RANK), lambda i: (i, 0)),
                   pl.BlockSpec((tm, ROPE_DIM), lambda i: (i, 0))],
        out_shape=[jax.ShapeDtypeStruct((rows, B_HEADS * QK_DIM), act_dtype),
                   jax.ShapeDtypeStruct((rows, QK_DIM), act_dtype),
                   jax.ShapeDtypeStruct((rows, KV_RANK), F32),
                   jax.ShapeDtypeStruct((rows, ROPE_DIM), F32)],
        compiler_params=_params(("arbitrary",)),
        name="mla_prep",
    )(proj_mla, cos, sin, g_q, g_kv, wq, wuk)


_EXP2_SCALE = MLA_SCALE * math.log2(math.e)


def _lane_blocks(x):
    return [x[:, k * LANES:(k + 1) * LANES] for k in range(x.shape[1] // LANES)]


def _softmax_step(s, v, m_ref, l_ref, acc_ref, rows=slice(None)):
    blocks = _lane_blocks(s)
    m_prev = m_ref[rows, :]
    m_new = jnp.maximum(m_prev, jnp.max(functools.reduce(jnp.maximum, blocks), axis=-1, keepdims=True))
    m_ref[rows, :] = m_new
    alpha = jnp.exp2((m_prev - m_new) * _EXP2_SCALE)
    ps = [jnp.exp2((blk - m_new) * _EXP2_SCALE) for blk in blocks]
    l_ref[rows, :] = alpha * l_ref[rows, :] + functools.reduce(jnp.add, ps)
    pv = _dot(jnp.concatenate(ps, axis=1).astype(BF), v)
    for k, blk in enumerate(_lane_blocks(pv)):
        cols = slice(k * LANES, (k + 1) * LANES)
        acc_ref[rows, cols] = alpha * acc_ref[rows, cols] + blk


def _softmax_finish(acc, l_part):
    return acc / jnp.sum(l_part, axis=-1, keepdims=True)


def _attn_p_body(q_ref, kv_ref, wuv_ref, o_ref, *scratch, tq, hpar):
    qi = pl.program_id(1)
    half = tq // 2
    stats = [scratch[3 * i:3 * i + 3] for i in range(hpar)]
    for h0 in range(0, B_HEADS, hpar):
        qs = [q_ref[0, :, (h0 + i) * QK_DIM:(h0 + i + 1) * QK_DIM] for i in range(hpar)]
        for acc_ref, m_ref, l_ref in stats:
            m_ref[...] = jnp.full_like(m_ref, -jnp.inf)
            l_ref[...] = jnp.zeros_like(l_ref)
            acc_ref[...] = jnp.zeros_like(acc_ref)

        def body(j, carry):
            kv = kv_ref[0, pl.ds(pl.multiple_of(j * tq, tq), tq), :]
            ss = [_dot_nt(qh, kv) for qh in qs]
            for s, (acc_ref, m_ref, l_ref) in zip(ss, stats):
                _softmax_step(s, kv[:, :KV_RANK], m_ref, l_ref, acc_ref)
            return carry

        lax.fori_loop(0, qi, body, 0)

        kv = kv_ref[0, pl.ds(pl.multiple_of(qi * tq, tq), tq), :]
        parts = [(slice(0, half), half), (slice(half, tq), tq)]
        ss = [[_dot_nt(qh[r], kv[:nk]) for r, nk in parts] for qh in qs]
        for s_parts, (acc_ref, m_ref, l_ref) in zip(ss, stats):
            for (r, nk), s in zip(parts, s_parts):
                qpos = lax.broadcasted_iota(jnp.int32, s.shape, 0) + r.start
                kpos = lax.broadcasted_iota(jnp.int32, s.shape, 1)
                s = jnp.where(kpos <= qpos, s, -jnp.inf)
                _softmax_step(s, kv[:nk, :KV_RANK], m_ref, l_ref, acc_ref, rows=r)
        for i, (acc_ref, m_ref, l_ref) in enumerate(stats):
            h = h0 + i
            o_lat = _softmax_finish(acc_ref[...], l_ref[...]).astype(BF)
            o_ref[0, :, h * V_DIM:(h + 1) * V_DIM] = _dot(o_lat, wuv_ref[h]).astype(o_ref.dtype)


def _attn_prompt(q_cat, kv_cat, wuv, b, t, tq=512, hpar=8):
    q3 = q_cat.reshape(b, t, q_cat.shape[-1])
    kv3 = kv_cat.reshape(b, t, kv_cat.shape[-1])
    body = functools.partial(_attn_p_body, tq=tq, hpar=hpar)
    return pl.pallas_call(
        body,
        grid=(b, t // tq),
        in_specs=[pl.BlockSpec((1, tq, q3.shape[-1]), lambda bi, qi: (bi, qi, 0)),
                  pl.BlockSpec((1, t, QK_DIM), lambda bi, qi: (bi, 0, 0)),
                  pl.BlockSpec(wuv.shape, lambda bi, qi: (0, 0, 0))],
        out_specs=pl.BlockSpec((1, tq, B_HEADS * V_DIM), lambda bi, qi: (bi, qi, 0)),
        out_shape=jax.ShapeDtypeStruct((b, t, B_HEADS * V_DIM), BF),
        scratch_shapes=[pltpu.VMEM((tq, KV_RANK), F32),
                        pltpu.VMEM((tq, LANES), F32),
                        pltpu.VMEM((tq, LANES), F32)] * hpar,
        compiler_params=_params(("arbitrary", "arbitrary")),
        name="attn_prompt",
    )(q3, kv3, wuv)


def _attn_s_body(pt_ref, q_ref, kvn_ref, ckv_hbm, kpe_hbm, wuv_ref, o_ref,
                 ckv_buf, kpe_buf, sem, qs_ref, acc_ref, m_ref, l_ref,
                 *, nch, ppc, n_pages, t_new, nslots):
    b = pl.program_id(0)
    ahead = nslots - 1

    def page_copies(bi, c, p):
        slot = c % nslots
        page = pt_ref[bi * n_pages + c * ppc + p]
        span = pl.ds(p * PAGE_SIZE, PAGE_SIZE)
        return (pltpu.make_async_copy(ckv_hbm.at[page], ckv_buf.at[slot, span, :], sem.at[slot, 0]),
                pltpu.make_async_copy(kpe_hbm.at[page], kpe_buf.at[slot, p], sem.at[slot, 1]))

    def start_all(bi, c):
        for p in range(ppc):
            for cp in page_copies(bi, c, p):
                cp.start()

    def wait_all(bi, c):
        for p in range(ppc):
            for cp in page_copies(bi, c, p):
                cp.wait()

    @pl.when(b == 0)
    def _():
        for c in range(ahead):
            start_all(b, c)

    for h in range(B_HEADS):
        qs_ref[h * t_new:(h + 1) * t_new, :] = q_ref[0, :, h * QK_DIM:(h + 1) * QK_DIM]
    m_ref[...] = jnp.full_like(m_ref, -jnp.inf)
    l_ref[...] = jnp.zeros_like(l_ref)
    acc_ref[...] = jnp.zeros_like(acc_ref)
    qs = qs_ref[...]
    qb = qs.astype(BF)
    q_lat = qb[:, :KV_RANK]
    q_pe = qb[:, KV_RANK:KV_RANK + ROPE_DIM]

    def scores(c):
        wait_all(b, c)
        slot = c % nslots
        kc = ckv_buf[slot].astype(BF)
        kpt = jnp.concatenate([kpe_buf[slot, p] for p in range(ppc)], axis=1).astype(BF)
        return kc, _dot_nt(q_lat, kc) + _dot(q_pe, kpt)

    cur = scores(0)
    for c in range(nch):
        nxt = c + ahead
        if nxt < nch:
            start_all(b, nxt)
        else:
            @pl.when(b + 1 < pl.num_programs(0))
            def _():
                start_all(b + 1, nxt - nch)
        following = scores(c + 1) if c + 1 < nch else None
        _softmax_step(cur[1], cur[0], m_ref, l_ref, acc_ref)
        cur = following

    kn = kvn_ref[0]
    rows = B_HEADS * t_new
    sn = _dot_nt(qs, kn)
    tq_ = lax.broadcasted_iota(jnp.int32, (rows, t_new), 0) % t_new
    tk_ = lax.broadcasted_iota(jnp.int32, (rows, t_new), 1)
    sn = jnp.where(tk_ <= tq_, sn, -jnp.inf)
    m_prev = m_ref[:, 0:1]
    m_new = jnp.maximum(m_prev, jnp.max(sn, axis=-1, keepdims=True))
    alpha = jnp.exp2((m_prev - m_new) * _EXP2_SCALE)
    pn = jnp.exp2((sn - m_new) * _EXP2_SCALE)
    l = alpha * jnp.sum(l_ref[...], axis=-1, keepdims=True) + jnp.sum(pn, axis=-1, keepdims=True)
    acc = alpha * acc_ref[...]
    vn = kn[:, :KV_RANK].astype(BF).astype(F32)
    pn = pn.astype(BF).astype(F32)
    for j in range(t_new):
        acc = acc + pn[:, j:j + 1] * vn[j:j + 1, :]
    o_lat = (acc / l).astype(BF)
    r = _dot(o_lat, wuv_ref[...])
    for h in range(B_HEADS):
        o_ref[0, :, h * V_DIM:(h + 1) * V_DIM] = r[h * t_new:(h + 1) * t_new, h * V_DIM:(h + 1) * V_DIM]


def _attn_sample(page_table, q_cat, kv_cat, cache_ckv, cache_kpe, wuv_all, nb, t_new, ppc=16, nslots=4):
    n_pages = page_table.shape[1]
    nch = n_pages // ppc
    assert nch % nslots == 0
    keys = ppc * PAGE_SIZE
    q3 = q_cat.reshape(nb, t_new, q_cat.shape[-1])
    kv3 = kv_cat.reshape(nb, t_new, kv_cat.shape[-1])
    rows = B_HEADS * t_new
    body = functools.partial(_attn_s_body, nch=nch, ppc=ppc, n_pages=n_pages, t_new=t_new, nslots=nslots)
    grid_spec = pltpu.PrefetchScalarGridSpec(
        num_scalar_prefetch=1,
        grid=(nb,),
        in_specs=[pl.BlockSpec((1, t_new, q3.shape[-1]), lambda bi, pt: (bi, 0, 0)),
                  pl.BlockSpec((1, t_new, QK_DIM), lambda bi, pt: (bi, 0, 0)),
                  pl.BlockSpec(memory_space=pl.ANY),
                  pl.BlockSpec(memory_space=pl.ANY),
                  pl.BlockSpec(wuv_all.shape, lambda bi, pt: (0, 0))],
        out_specs=pl.BlockSpec((1, t_new, B_HEADS * V_DIM), lambda bi, pt: (bi, 0, 0)),
        scratch_shapes=[pltpu.VMEM((nslots, keys, KV_RANK), F32),
                        pltpu.VMEM((nslots, ppc, ROPE_DIM, PAGE_SIZE), F32),
                        pltpu.SemaphoreType.DMA((nslots, 2)),
                        pltpu.VMEM((rows, QK_DIM), F32),
                        pltpu.VMEM((rows, KV_RANK), F32),
                        pltpu.VMEM((rows, LANES), F32),
                        pltpu.VMEM((rows, LANES), F32)])
    return pl.pallas_call(
        body,
        grid_spec=grid_spec,
        out_shape=jax.ShapeDtypeStruct((nb, t_new, B_HEADS * V_DIM), F32),
        compiler_params=_params(("arbitrary",)),
        name="attn_sample",
    )(page_table.reshape(-1), q3, kv3, cache_ckv, cache_kpe, wuv_all)


def _mix_body(oa_ref, ob_ref, ga_ref, gb_ref, x_ref, gt_ref, g_ref, wa_ref, wb_ref, wo_ref, o_ref):
    ya = _dot(oa_ref[...].astype(BF), wa_ref[...])
    yb = _dot(ob_ref[...].astype(BF), wb_ref[...])
    merged = jax.nn.sigmoid(ga_ref[...]) * ya + jax.nn.sigmoid(gb_ref[...]) * yb
    z = _dot(merged.astype(BF), wo_ref[...])
    x = x_ref[...]
    o_ref[...] = x + gt_ref[...] * _rms(z, g_ref[...]).reshape(x.shape)


def _mix(o_a, o_b, proj, x, mod, g, wa, wb, wo, gb, tr):
    bm, r, d = x.shape
    tm = gb * tr
    nr = r // tr
    gate0 = 4 * A_HEADS * A_DK // d

    def rows2(width, cb):
        return pl.BlockSpec((tm, width), lambda a, i: (a * nr + i, cb))

    def const(w):
        return pl.BlockSpec(w.shape, lambda a, i: (0, 0))

    return pl.pallas_call(
        _mix_body,
        grid=(bm // gb, nr),
        in_specs=[rows2(o_a.shape[1], 0), rows2(o_b.shape[1], 0),
                  rows2(d, gate0), rows2(d, gate0 + 1),
                  pl.BlockSpec((gb, tr, d), lambda a, i: (a, i, 0)),
                  pl.BlockSpec((gb, 1, d), lambda a, i: (a, 0, 2)),
                  const(g), const(wa), const(wb), const(wo)],
        out_specs=pl.BlockSpec((gb, tr, d), lambda a, i: (a, i, 0)),
        out_shape=jax.ShapeDtypeStruct(x.shape, F32),
        compiler_params=_params(("arbitrary", "arbitrary")),
        name="mix",
    )(o_a, o_b, proj, proj, x, mod, g, wa, wb, wo)


def _mlp_body(x_ref, sh_ref, sc_ref, gt_ref, g1_ref, g2_ref, wu_ref, wd_ref, o_ref, h_ref, acc_ref):
    f = pl.program_id(2)

    @pl.when(f == 0)
    def _():
        x = x_ref[...]
        h = _rms(x, g1_ref[...]) * (1.0 + sc_ref[...]) + sh_ref[...]
        h_ref[...] = h.reshape(h_ref.shape).astype(BF)
        acc_ref[...] = jnp.zeros_like(acc_ref)

    u = jnp.maximum(_dot(h_ref[...], wu_ref[...]), 0.0)
    acc_ref[...] += _dot((u * u).astype(BF), wd_ref[...])

    @pl.when(f == pl.num_programs(2) - 1)
    def _():
        x = x_ref[...]
        o_ref[...] = x + gt_ref[...] * _rms(acc_ref[...], g2_ref[...]).reshape(x.shape)


def _mlp(x, mod, g1, g2, wu, wd, gb, tr, tf=1024):
    bm, r, d = x.shape
    tm = gb * tr
    nr = r // tr
    ff = wu.shape[1]

    def modspec(piece):
        return pl.BlockSpec((gb, 1, d), lambda a, i, f: (a, 0, piece))

    return pl.pallas_call(
        _mlp_body,
        grid=(bm // gb, nr, ff // tf),
        in_specs=[pl.BlockSpec((gb, tr, d), lambda a, i, f: (a, i, 0)),
                  modspec(3), modspec(4), modspec(5),
                  pl.BlockSpec((1, d), lambda a, i, f: (0, 0)),
                  pl.BlockSpec((1, d), lambda a, i, f: (0, 0)),
                  pl.BlockSpec((d, tf), lambda a, i, f: (0, f)),
                  pl.BlockSpec((tf, d), lambda a, i, f: (f, 0))],
        out_specs=pl.BlockSpec((gb, tr, d), lambda a, i, f: (a, i, 0)),
        out_shape=jax.ShapeDtypeStruct(x.shape, F32),
        scratch_shapes=[pltpu.VMEM((tm, d), BF), pltpu.VMEM((tm, d), F32)],
        compiler_params=_params(("arbitrary", "arbitrary", "arbitrary")),
        name="mlp",
    )(x, mod, mod, mod, g1, g2, wu, wd)


def _rope_tables(pos):
    half = ROPE_DIM // 2
    inv = ROPE_THETA ** (-jnp.arange(half, dtype=F32) / half)
    ang = pos.astype(F32)[:, None] * inv[None, :]
    z = jnp.zeros((pos.shape[0], LANES - ROPE_DIM), F32)
    cos = jnp.cos(ang)
    sin = jnp.sin(ang)
    return jnp.concatenate([cos, cos, z], axis=1), jnp.concatenate([sin, sin, z], axis=1)


def _swap_halves(w):
    half = w.shape[-1] // 2
    return jnp.concatenate([-w[..., half:], w[..., :half]], axis=-1)


def _pad_lanes(w):
    pad = [(0, 0)] * (w.ndim - 1) + [(0, LANES - w.shape[-1])]
    return jnp.pad(w, pad)


def kernel(x_prompt, x_sample, c_prompt, c_sample, cache_ckv, cache_kpe, state_hgrn, page_table, w_ada, b_ada, g_pre_mix, g_post_mix, g_pre_mlp, g_post_mlp, w_in, lb_logits, g_hgrn_norm, w_a_out, g_q_norm, w_q_up, g_kv_norm, w_kv_up, w_b_out, w_o, w_up, w_down):
    depth = w_in.shape[0]
    assert depth == 1
    batch, seq, d = x_prompt.shape
    nb, t_new, _ = x_sample.shape
    past_len = page_table.shape[1] * PAGE_SIZE
    hk = A_HEADS * A_DK

    wit = w_in[0].T
    o_qd = 3 * hk + A_HEADS * A_DV
    o_kpe = o_qd + Q_RANK + KV_RANK
    o_gate = o_kpe + ROPE_DIM
    wit = wit.astype(BF)
    w_kpe = wit[o_kpe:o_gate]
    half = ROPE_DIM // 2
    w_kpe_swapped = jnp.concatenate([-w_kpe[half:], w_kpe[:half]], axis=0)
    zpad = jnp.zeros((LANES - ROPE_DIM, d), BF)
    w_mla = jnp.concatenate([wit[o_qd:o_kpe], w_kpe, zpad, w_kpe_swapped, zpad], axis=0)
    wq = w_q_up[0].reshape(Q_RANK, B_HEADS, NOPE_DIM + ROPE_DIM)
    wq_pe = wq[..., NOPE_DIM:]
    wq_cat = jnp.concatenate([wq[..., :NOPE_DIM].reshape(Q_RANK, -1),
                              _pad_lanes(wq_pe).reshape(Q_RANK, -1),
                              _pad_lanes(_swap_halves(wq_pe)).reshape(Q_RANK, -1)], axis=1).astype(BF)
    wkv = w_kv_up[0].reshape(KV_RANK, B_HEADS, NOPE_DIM + V_DIM)
    wuk = wkv[..., :NOPE_DIM].transpose(1, 2, 0).astype(BF)
    wuv = wkv[..., NOPE_DIM:].transpose(1, 0, 2).astype(BF)
    wuv_all = wkv[..., NOPE_DIM:].reshape(KV_RANK, B_HEADS * V_DIM).astype(BF)
    wa = w_a_out[0].astype(BF)
    wb = w_b_out[0].astype(BF)
    wo = w_o[0].astype(BF)
    wu = w_up[0].astype(BF)
    wd = w_down[0].astype(BF)

    c_all = jnp.concatenate([c_prompt, c_sample], axis=0)
    mod = _ada(c_all, w_ada[0], b_ada[0][None, :])
    mod = mod.reshape(batch + nb, 1, 6 * d)
    mod_p, mod_s = mod[:batch], mod[batch:]

    cos_p, sin_p = _rope_tables(jnp.arange(seq))
    cos_s, sin_s = _rope_tables(past_len + jnp.arange(t_new))
    gs = _TILE
    reps = gs
    cos_s = jnp.tile(cos_s, (reps, 1))
    sin_s = jnp.tile(sin_s, (reps, 1))

    def layer(x, mod_g, gb, tr, hgrn, attend, cos, sin, n_pos_tiles, act_dtype):
        tm = gb * tr
        proj = _proj(x, mod_g, g_pre_mix, wit, gb, tr, 2048, skip=(o_qd, o_gate - o_qd))
        proj_mla = _proj(x, mod_g, g_pre_mix, w_mla, gb, tr, w_mla.shape[0])
        o_a, s_fin = hgrn(proj)
        q_cat, kv_cat, ckv, kpe = _mla_prep(proj_mla, cos, sin, g_q_norm, g_kv_norm, wq_cat, wuk,
                                            tm, n_pos_tiles, act_dtype)
        o_b = attend(q_cat, kv_cat)
        mix_gb, mix_tr = (1, tr // 2) if gb == 1 else (gb // 2, tr)
        x1 = _mix(o_a.reshape(-1, o_a.shape[-1]), o_b.reshape(-1, o_b.shape[-1]), proj, x, mod_g,
                  g_post_mix, wa, wb, wo, mix_gb, mix_tr)
        y = _mlp(x1, mod_g, g_pre_mlp, g_post_mlp, wu, wd, gb, tr)
        return y, ckv, kpe, s_fin

    kpe_pages_t = jnp.swapaxes(cache_kpe[0], 1, 2)

    tr_p = 512
    y_p, ckv_p, kpe_p, s_p = layer(
        x_prompt, mod_p, 1, tr_p,
        lambda proj: _hgrn_prompt(proj, lb_logits, g_hgrn_norm, batch, seq),
        lambda q, kv: _attn_prompt(q, kv, wuv, batch, seq),
        cos_p, sin_p, seq // tr_p, BF)
    y_s, ckv_s, kpe_s, s_s = layer(
        x_sample, mod_s, gs, t_new,
        lambda proj: _hgrn_sample(proj, lb_logits, g_hgrn_norm, state_hgrn, nb, t_new),
        lambda q, kv: _attn_sample(page_table, q, kv, cache_ckv[0], kpe_pages_t, wuv_all, nb, t_new),
        cos_s, sin_s, 1, F32)

    return (y_p, y_s,
            ckv_p.reshape(1, batch, seq, KV_RANK), kpe_p.reshape(1, batch, seq, ROPE_DIM),
            s_p[None],
            ckv_s.reshape(1, nb, t_new, KV_RANK), kpe_s.reshape(1, nb, t_new, ROPE_DIM),
            s_s)
```

```python
import functools
import math

import numpy as np
import jax
import jax.numpy as jnp
from jax import lax
from jax.experimental import pallas as pl
from jax.experimental.pallas import tpu as pltpu

BF = jnp.bfloat16
F32 = jnp.float32

D_MODEL = 2048
A_HEADS = 8
A_DK = 128
A_DV = 128
A_CHUNK = 64
B_HEADS = 8
Q_RANK = 512
KV_RANK = 512
NOPE_DIM = 128
ROPE_DIM = 64
V_DIM = 128
ROPE_THETA = 10000.0
MLA_SCALE = (NOPE_DIM + ROPE_DIM) ** -0.5
D_FF = 4 * D_MODEL
EPS = 1e-6
PAGE_SIZE = 128

LANES = 128
QK_DIM = KV_RANK + LANES
VMEM_LIMIT = 56 * 1024 * 1024

_NT = (((1,), (1,)), ((), ()))
_TN = (((0,), (0,)), ((), ()))


def _dot(a, b):
    return jnp.dot(a, b, preferred_element_type=F32)


def _dot_nt(a, b):
    return lax.dot_general(a, b, _NT, preferred_element_type=F32)


def _dot_tn(a, b):
    return lax.dot_general(a, b, _TN, preferred_element_type=F32)


def _rms(x, g):
    r = lax.rsqrt(jnp.mean(x * x, axis=-1, keepdims=True) + EPS)
    return (x * r) * g


def _params(sem):
    return pltpu.CompilerParams(dimension_semantics=sem, vmem_limit_bytes=VMEM_LIMIT)


def _ada_body(c_ref, w_ref, b_ref, o_ref):
    c = c_ref[...]
    s = (c * jax.nn.sigmoid(c)).astype(BF)
    o_ref[...] = _dot(s, w_ref[...].astype(BF)) + b_ref[...]


def _ada(c_all, w_ada, b_ada, tn=1024):
    m, d = c_all.shape
    n = w_ada.shape[1]
    return pl.pallas_call(
        _ada_body,
        grid=(n // tn,),
        in_specs=[pl.BlockSpec((m, d), lambda j: (0, 0)),
                  pl.BlockSpec((d, tn), lambda j: (0, j)),
                  pl.BlockSpec((1, tn), lambda j: (0, j))],
        out_specs=pl.BlockSpec((m, tn), lambda j: (0, j)),
        out_shape=jax.ShapeDtypeStruct((m, n), F32),
        compiler_params=_params(("arbitrary",)),
        name="ada",
    )(c_all, w_ada, b_ada)


def _proj_body(x_ref, sh_ref, sc_ref, g_ref, w_ref, o_ref, h_ref):
    @pl.when(pl.program_id(2) == 0)
    def _():
        x = x_ref[...]
        h = _rms(x, g_ref[...]) * (1.0 + sc_ref[...]) + sh_ref[...]
        h_ref[...] = h.reshape(h_ref.shape).astype(BF)

    o_ref[...] = _dot_nt(h_ref[...], w_ref[...])


def _proj(x, mod, g, w_t, gb, tr, tn, skip=None):
    bm, r, d = x.shape
    tm = gb * tr
    nr = r // tr
    if skip is None:
        n = w_t.shape[0]
        w_spec = pl.BlockSpec((tn, d), lambda a, i, j: (j, 0))
    else:
        first, gap = skip[0] // tn, skip[1]
        n = w_t.shape[0] - gap
        w_spec = pl.BlockSpec((pl.Element(tn), pl.Element(d)),
                              lambda a, i, j: (pl.multiple_of(j * tn + jnp.where(j >= first, gap, 0),
                                                              math.gcd(tn, gap)), 0))
    return pl.pallas_call(
        _proj_body,
        grid=(bm // gb, nr, n // tn),
        in_specs=[pl.BlockSpec((gb, tr, d), lambda a, i, j: (a, i, 0)),
                  pl.BlockSpec((gb, 1, d), lambda a, i, j: (a, 0, 0)),
                  pl.BlockSpec((gb, 1, d), lambda a, i, j: (a, 0, 1)),
                  pl.BlockSpec((1, d), lambda a, i, j: (0, 0)),
                  w_spec],
        out_specs=pl.BlockSpec((tm, tn), lambda a, i, j: (a * nr + i, j)),
        out_shape=jax.ShapeDtypeStruct((bm * r, n), F32),
        scratch_shapes=[pltpu.VMEM((tm, d), BF)],
        compiler_params=_params(("arbitrary", "arbitrary", "arbitrary")),
        name="proj",
    )(x, mod, mod, g, w_t)


_TILE = 64


def _hgrn_consts(chunk):
    t = np.arange(_TILE)[:, None]
    u = np.arange(_TILE)[None, :]
    tril = ((t // chunk) == (u // chunk)) & (u <= t)
    masks = []
    m = chunk
    while m >= 2:
        h = m // 2
        masks.append(((t // m) == (u // m)) & ((t % m) >= h) & ((u % m) < h))
        m = h
    masks.append(t == u)
    mk = np.concatenate([a.astype(np.float32) for a in masks], axis=0)
    return jnp.asarray(tril.astype(np.float32), BF), jnp.asarray(mk, F32)


def _split3(x):
    hi = x.astype(BF)
    r1 = x - hi.astype(F32)
    mid = r1.astype(BF)
    lo = (r1 - mid.astype(F32)).astype(BF)
    return hi, mid, lo


def _hgrn_gates(qa, fa, lb):
    q = qa * jax.nn.sigmoid(qa)
    f = lb + (1.0 - lb) * jax.nn.sigmoid(fa)
    return q, jnp.log(f), 1.0 - f


def _hgrn_lb(lbl):
    e = jnp.exp(lbl - jnp.max(lbl, axis=0, keepdims=True))
    return e[0:1] / jnp.sum(e, axis=0, keepdims=True)


def _hgrn_scores(q, k, levels, mk_ref):
    nlev = len(levels)
    sc = jnp.where(mk_ref[nlev * _TILE:(nlev + 1) * _TILE, :] > 0.5,
                   _dot_nt(q.astype(BF), k.astype(BF)), 0.0)
    row = lax.broadcasted_iota(jnp.int32, q.shape, 0)
    for l, e in enumerate(levels):
        m = (1 << nlev) >> l
        x = (jnp.where((row & (m // 2)) != 0, q, k) * e).astype(BF)
        p = _dot_nt(x, x)
        sc = sc + jnp.where(mk_ref[l * _TILE:(l + 1) * _TILE, :] > 0.5, p, 0.0)
    return sc


def _row_bcast(x, rows, n):
    w = x.shape[1]
    return jnp.concatenate([jnp.broadcast_to(x[r:r + 1, :], (n, w)) for r in rows], axis=0)


def _hgrn_decays(tril_ref, logf, chunk):
    w = logf.shape[1]
    a = _dot(tril_ref[...], jnp.concatenate(_split3(logf), axis=1))
    b = (a[:, :w] + a[:, w:2 * w] + a[:, 2 * w:]) * math.log2(math.e)
    sub = lax.broadcasted_iota(jnp.int32, (8, w), 0)

    def refs_in_groups(offs):
        span = 8 // len(offs)
        pieces = []
        for g in range(_TILE // 8):
            rows = [jnp.broadcast_to(b[g * 8 + o:g * 8 + o + 1, :], (8, w)) for o in offs]
            ref = rows[-1]
            for i in range(len(offs) - 2, -1, -1):
                ref = jnp.where(sub < (i + 1) * span, rows[i], ref)
            pieces.append(ref)
        return jnp.concatenate(pieces, axis=0)

    b_last = _row_bcast(b, range(chunk - 1, _TILE, chunk), chunk)
    levels = []
    m = chunk
    while m >= 2:
        if m >= 16:
            ref = _row_bcast(b, range(m // 2, _TILE, m), m)
        else:
            ref = refs_in_groups(list(range(m // 2, 8, m)))
        levels.append(jnp.exp2(-jnp.abs(b - ref)))
        m //= 2
    return jnp.exp2(b), jnp.exp2(b_last - b), levels


def _hgrn_out(o, ga, gn):
    return (_rms(o, gn) * jax.nn.sigmoid(ga)).astype(BF)


def _hgrn_p_body(qa_ref, fa_ref, ia_ref, ga_ref, lb_ref, gn_ref, nm_ref, mk_ref,
                 o_ref, s_ref, st_ref, *, nchunks, chunk_len, nh):
    ti = pl.program_id(2)

    @pl.when(ti == 0)
    def _():
        st_ref[...] = jnp.zeros_like(st_ref)

    lb = _hgrn_lb(lb_ref[...])
    gn = gn_ref[...]

    def chunk(c, carry):
        rows = pl.ds(pl.multiple_of(c * _TILE, _TILE), _TILE)
        qa, fa, ia, ga = (r[0, rows, :] for r in (qa_ref, fa_ref, ia_ref, ga_ref))
        sts = [st_ref[j] for j in range(nh)]
        hs = range(nh)
        cols = [slice(j * A_DK, (j + 1) * A_DK) for j in hs]
        q, logf, k = _hgrn_gates(qa, fa, lb)
        eb, er, levels = _hgrn_decays(nm_ref, logf, chunk_len)
        qe = (q * eb).astype(BF)
        ke = (k * er).astype(BF)
        vs = [ia[:, c].astype(BF) for c in cols]
        o_inter = [_dot_nt(qe[:, c], sts[j].astype(BF)) for j, c in enumerate(cols)]
        scs = [_hgrn_scores(q[:, c], k[:, c], [e[:, c] for e in levels], mk_ref) for c in cols]
        os_ = [o_inter[j] + _dot(scs[j].astype(BF), vs[j]) for j in hs]
        upd = [_dot_tn(vs[j], ke[:, c]) for j, c in enumerate(cols)]
        for j, c in enumerate(cols):
            st_ref[j] = sts[j] * eb[_TILE - 1:_TILE, c] + upd[j]
        o_ref[0, rows, :] = jnp.concatenate(
            [_hgrn_out(os_[j], ga[:, c], gn[:, c]) for j, c in enumerate(cols)], axis=1)
        return carry

    lax.fori_loop(0, nchunks, chunk, 0)

    @pl.when(ti == pl.num_programs(2) - 1)
    def _():
        for j in range(nh):
            s_ref[0, j] = st_ref[j].T


def _hgrn_prompt(proj, lb_logits, g_norm, b, t, tt=1024, nh=8):
    nm, mk = _hgrn_consts(A_CHUNK)
    h = A_HEADS
    hg = h // nh
    proj3 = proj.reshape(b, t, proj.shape[-1])

    def col(off):
        return pl.BlockSpec((1, tt, nh * A_DK), lambda bi, hi, ti: (bi, ti, off + hi))

    body = functools.partial(_hgrn_p_body, nchunks=tt // _TILE, chunk_len=A_CHUNK, nh=nh)
    return pl.pallas_call(
        body,
        grid=(b, hg, t // tt),
        in_specs=[col(0), col(hg), col(2 * hg), col(3 * hg),
                  pl.BlockSpec((lb_logits.shape[0], nh * A_DK), lambda bi, hi, ti: (0, hi)),
                  pl.BlockSpec((1, nh * A_DV), lambda bi, hi, ti: (0, hi)),
                  pl.BlockSpec(nm.shape, lambda bi, hi, ti: (0, 0)),
                  pl.BlockSpec(mk.shape, lambda bi, hi, ti: (0, 0))],
        out_specs=[pl.BlockSpec((1, tt, nh * A_DV), lambda bi, hi, ti: (bi, ti, hi)),
                   pl.BlockSpec((1, nh, A_DK, A_DV), lambda bi, hi, ti: (bi, hi, 0, 0))],
        out_shape=[jax.ShapeDtypeStruct((b, t, h * A_DV), BF),
                   jax.ShapeDtypeStruct((b, h, A_DK, A_DV), F32)],
        scratch_shapes=[pltpu.VMEM((nh, A_DV, A_DK), F32)],
        compiler_params=_params(("arbitrary", "arbitrary", "arbitrary")),
        name="hgrn_prompt",
    )(proj3, proj3, proj3, proj3, lb_logits, g_norm, nm, mk)


def _hgrn_s_body(qa_ref, fa_ref, ia_ref, ga_ref, lb_ref, gn_ref, nm_ref, mk_ref, sel_ref,
                 s0_ref, o_ref, s_ref, *, chunk_len, gb, nh):
    lb = _hgrn_lb(lb_ref[...])
    q_all, logf_all, k_all = _hgrn_gates(qa_ref[...], fa_ref[...], lb)
    eb_all, er_all, levels_all = _hgrn_decays(nm_ref, logf_all, chunk_len)
    sel = sel_ref[...]
    selb = sel.astype(BF)
    gn = gn_ref[...]
    outs = []
    for hh in range(nh):
        hc = slice(hh * A_DK, (hh + 1) * A_DK)
        q, logf, k = q_all[:, hc], logf_all[:, hc], k_all[:, hc]
        vf = ia_ref[:, hc]
        v = vf.astype(BF)
        parts = _split3(logf)
        dec = jnp.exp(_dot_tn(jnp.concatenate(parts, axis=0), jnp.concatenate([selb] * len(parts), axis=0)))
        qe = q * eb_all[:, hc]
        q_blk = (jnp.concatenate([qe] * gb, axis=1) * sel).astype(BF)
        v_blk = (jnp.concatenate([vf] * gb, axis=1) * sel).astype(BF)
        s0 = s0_ref[0, :, hh]
        o = _dot(q_blk, s0.reshape(gb * A_DK, A_DV).astype(BF))
        sc = _hgrn_scores(q, k, [e[:, hc] for e in levels_all], mk_ref)
        o = o + _dot(sc.astype(BF), v)
        upd = _dot_tn((k * er_all[:, hc]).astype(BF), v_blk)
        for j in range(gb):
            cols = slice(j * A_DV, (j + 1) * A_DV)
            s_ref[0, j, hh] = dec[:, cols] * s0[j] + upd[:, cols]
        outs.append(_hgrn_out(o, ga_ref[:, hc], gn[:, hc]))
    o_ref[...] = jnp.concatenate(outs, axis=1)


def _hgrn_sample(proj, lb_logits, g_norm, state, nb, t, nh=8):
    gb = _TILE // t
    nm, mk = _hgrn_consts(t)
    h = A_HEADS
    hg = h // nh
    sel = np.zeros((_TILE, gb * A_DV), np.float32)
    for j in range(gb):
        sel[j * t:(j + 1) * t, j * A_DV:(j + 1) * A_DV] = 1.0
    sel = jnp.asarray(sel)

    def col(off):
        return pl.BlockSpec((_TILE, nh * A_DK), lambda i, hi: (i, off + hi))

    body = functools.partial(_hgrn_s_body, chunk_len=t, gb=gb, nh=nh)
    st_spec = pl.BlockSpec((1, gb, nh, A_DK, A_DV), lambda i, hi: (0, i, hi, 0, 0))
    return pl.pallas_call(
        body,
        grid=(nb // gb, hg),
        in_specs=[col(0), col(hg), col(2 * hg), col(3 * hg),
                  pl.BlockSpec((lb_logits.shape[0], nh * A_DK), lambda i, hi: (0, hi)),
                  pl.BlockSpec((1, nh * A_DV), lambda i, hi: (0, hi)),
                  pl.BlockSpec(nm.shape, lambda i, hi: (0, 0)),
                  pl.BlockSpec(mk.shape, lambda i, hi: (0, 0)),
                  pl.BlockSpec(sel.shape, lambda i, hi: (0, 0)),
                  st_spec],
        out_specs=[pl.BlockSpec((_TILE, nh * A_DV), lambda i, hi: (i, hi)), st_spec],
        out_shape=[jax.ShapeDtypeStruct((nb * t, h * A_DV), BF),
                   jax.ShapeDtypeStruct(state.shape, F32)],
        compiler_params=_params(("arbitrary", "arbitrary")),
        name="hgrn_sample",
    )(proj, proj, proj, proj, lb_logits, g_norm, nm, mk, sel, state)


def _mla_prep_body(p_ref, cos_ref, sin_ref, gq_ref, gkv_ref, wq_ref, wuk_ref,
                   q_ref, kv_ref, ckv_ref, kpe_ref):
    p = p_ref[...]
    cos = cos_ref[...]
    sin = sin_ref[...]
    qn = _rms(p[:, :Q_RANK], gq_ref[...]).astype(BF)
    qf = _dot(qn, wq_ref[...])
    hw = B_HEADS * NOPE_DIM
    for h in range(B_HEADS):
        c0 = h * NOPE_DIM
        q_lat = _dot(qf[:, c0:c0 + NOPE_DIM].astype(BF), wuk_ref[h])
        q_pe = qf[:, hw + c0:hw + c0 + LANES] * cos + qf[:, 2 * hw + c0:2 * hw + c0 + LANES] * sin
        q_ref[:, h * QK_DIM:h * QK_DIM + KV_RANK] = q_lat.astype(q_ref.dtype)
        q_ref[:, h * QK_DIM + KV_RANK:(h + 1) * QK_DIM] = q_pe.astype(q_ref.dtype)
    ckv = _rms(p[:, Q_RANK:Q_RANK + KV_RANK], gkv_ref[...])
    o = Q_RANK + KV_RANK
    kpe = p[:, o:o + LANES] * cos + p[:, o + LANES:o + 2 * LANES] * sin
    ckv_ref[...] = ckv
    kpe_ref[...] = kpe[:, :ROPE_DIM]
    kv_ref[:, :KV_RANK] = ckv.astype(kv_ref.dtype)
    kv_ref[:, KV_RANK:] = kpe.astype(kv_ref.dtype)


def _mla_prep(proj_mla, cos, sin, g_q, g_kv, wq, wuk, tm, n_pos_tiles, act_dtype):
    rows, pc = proj_mla.shape
    return pl.pallas_call(
        _mla_prep_body,
        grid=(rows // tm,),
        in_specs=[pl.BlockSpec((tm, pc), lambda i: (i, 0)),
                  pl.BlockSpec((tm, LANES), lambda i: (i % n_pos_tiles, 0)),
                  pl.BlockSpec((tm, LANES), lambda i: (i % n_pos_tiles, 0)),
                  pl.BlockSpec((1, Q_RANK), lambda i: (0, 0)),
                  pl.BlockSpec((1, KV_RANK), lambda i: (0, 0)),
                  pl.BlockSpec(wq.shape, lambda i: (0, 0)),
                  pl.BlockSpec(wuk.shape, lambda i: (0, 0, 0))],
        out_specs=[pl.BlockSpec((tm, B_HEADS * QK_DIM), lambda i: (i, 0)),
                   pl.BlockSpec((tm, QK_DIM), lambda i: (i, 0)),
                   pl.BlockSpec((tm, KV_RANK), lambda i: (i, 0)),
                   pl.BlockSpec((tm, ROPE_DIM), lambda i: (i, 0))],
        out_shape=[jax.ShapeDtypeStruct((rows, B_HEADS * QK_DIM), act_dtype),
                   jax.ShapeDtypeStruct((rows, QK_DIM), act_dtype),
                   jax.ShapeDtypeStruct((rows, KV_RANK), F32),
                   jax.ShapeDtypeStruct((rows, ROPE_DIM), F32)],
        compiler_params=_params(("arbitrary",)),
        name="mla_prep",
    )(proj_mla, cos, sin, g_q, g_kv, wq, wuk)


_EXP2_SCALE = MLA_SCALE * math.log2(math.e)


def _lane_blocks(x):
    return [x[:, k * LANES:(k + 1) * LANES] for k in range(x.shape[1] // LANES)]


def _softmax_step(s, v, m_ref, l_ref, acc_ref, rows=slice(None)):
    blocks = _lane_blocks(s)
    m_prev = m_ref[rows, :]
    m_new = jnp.maximum(m_prev, jnp.max(functools.reduce(jnp.maximum, blocks), axis=-1, keepdims=True))
    m_ref[rows, :] = m_new
    alpha = jnp.exp2((m_prev - m_new) * _EXP2_SCALE)
    ps = [jnp.exp2((blk - m_new) * _EXP2_SCALE) for blk in blocks]
    l_ref[rows, :] = alpha * l_ref[rows, :] + functools.reduce(jnp.add, ps)
    pv = _dot(jnp.concatenate(ps, axis=1).astype(BF), v)
    for k, blk in enumerate(_lane_blocks(pv)):
        cols = slice(k * LANES, (k + 1) * LANES)
        acc_ref[rows, cols] = alpha * acc_ref[rows, cols] + blk


def _softmax_finish(acc, l_part):
    return acc / jnp.sum(l_part, axis=-1, keepdims=True)


def _attn_p_body(q_ref, kv_ref, wuv_ref, o_ref, *scratch, tq, hpar):
    qi = pl.program_id(1)
    half = tq // 2
    stats = [scratch[3 * i:3 * i + 3] for i in range(hpar)]
    for h0 in range(0, B_HEADS, hpar):
        qs = [q_ref[0, :, (h0 + i) * QK_DIM:(h0 + i + 1) * QK_DIM] for i in range(hpar)]
        for acc_ref, m_ref, l_ref in stats:
            m_ref[...] = jnp.full_like(m_ref, -jnp.inf)
            l_ref[...] = jnp.zeros_like(l_ref)
            acc_ref[...] = jnp.zeros_like(acc_ref)

        def body(j, carry):
            kv = kv_ref[0, pl.ds(pl.multiple_of(j * tq, tq), tq), :]
            ss = [_dot_nt(qh, kv) for qh in qs]
            for s, (acc_ref, m_ref, l_ref) in zip(ss, stats):
                _softmax_step(s, kv[:, :KV_RANK], m_ref, l_ref, acc_ref)
            return carry

        lax.fori_loop(0, qi, body, 0)

        kv = kv_ref[0, pl.ds(pl.multiple_of(qi * tq, tq), tq), :]
        parts = [(slice(0, half), half), (slice(half, tq), tq)]
        ss = [[_dot_nt(qh[r], kv[:nk]) for r, nk in parts] for qh in qs]
        for s_parts, (acc_ref, m_ref, l_ref) in zip(ss, stats):
            for (r, nk), s in zip(parts, s_parts):
                qpos = lax.broadcasted_iota(jnp.int32, s.shape, 0) + r.start
                kpos = lax.broadcasted_iota(jnp.int32, s.shape, 1)
                s = jnp.where(kpos <= qpos, s, -jnp.inf)
                _softmax_step(s, kv[:nk, :KV_RANK], m_ref, l_ref, acc_ref, rows=r)
        for i, (acc_ref, m_ref, l_ref) in enumerate(stats):
            h = h0 + i
            o_lat = _softmax_finish(acc_ref[...], l_ref[...]).astype(BF)
            o_ref[0, :, h * V_DIM:(h + 1) * V_DIM] = _dot(o_lat, wuv_ref[h]).astype(o_ref.dtype)


def _attn_prompt(q_cat, kv_cat, wuv, b, t, tq=512, hpar=8):
    q3 = q_cat.reshape(b, t, q_cat.shape[-1])
    kv3 = kv_cat.reshape(b, t, kv_cat.shape[-1])
    body = functools.partial(_attn_p_body, tq=tq, hpar=hpar)
    return pl.pallas_call(
        body,
        grid=(b, t // tq),
        in_specs=[pl.BlockSpec((1, tq, q3.shape[-1]), lambda bi, qi: (bi, qi, 0)),
                  pl.BlockSpec((1, t, QK_DIM), lambda bi, qi: (bi, 0, 0)),
                  pl.BlockSpec(wuv.shape, lambda bi, qi: (0, 0, 0))],
        out_specs=pl.BlockSpec((1, tq, B_HEADS * V_DIM), lambda bi, qi: (bi, qi, 0)),
        out_shape=jax.ShapeDtypeStruct((b, t, B_HEADS * V_DIM), BF),
        scratch_shapes=[pltpu.VMEM((tq, KV_RANK), F32),
                        pltpu.VMEM((tq, LANES), F32),
                        pltpu.VMEM((tq, LANES), F32)] * hpar,
        compiler_params=_params(("arbitrary", "arbitrary")),
        name="attn_prompt",
    )(q3, kv3, wuv)


def _attn_s_body(pt_ref, q_ref, kvn_ref, ckv_hbm, kpe_hbm, wuv_ref, o_ref,
                 ckv_buf, kpe_buf, sem, qs_ref, acc_ref, m_ref, l_ref,
                 *, nch, ppc, n_pages, t_new, nslots):
    b = pl.program_id(0)
    ahead = nslots - 1

    def page_copies(bi, c, p):
        slot = c % nslots
        page = pt_ref[bi * n_pages + c * ppc + p]
        span = pl.ds(p * PAGE_SIZE, PAGE_SIZE)
        return (pltpu.make_async_copy(ckv_hbm.at[page], ckv_buf.at[slot, span, :], sem.at[slot, 0]),
                pltpu.make_async_copy(kpe_hbm.at[page], kpe_buf.at[slot, p], sem.at[slot, 1]))

    def start_all(bi, c):
        for p in range(ppc):
            for cp in page_copies(bi, c, p):
                cp.start()

    def wait_all(bi, c):
        for p in range(ppc):
            for cp in page_copies(bi, c, p):
                cp.wait()

    @pl.when(b == 0)
    def _():
        for c in range(ahead):
            start_all(b, c)

    for h in range(B_HEADS):
        qs_ref[h * t_new:(h + 1) * t_new, :] = q_ref[0, :, h * QK_DIM:(h + 1) * QK_DIM]
    m_ref[...] = jnp.full_like(m_ref, -jnp.inf)
    l_ref[...] = jnp.zeros_like(l_ref)
    acc_ref[...] = jnp.zeros_like(acc_ref)
    qs = qs_ref[...]
    qb = qs.astype(BF)
    q_lat = qb[:, :KV_RANK]
    q_pe = qb[:, KV_RANK:KV_RANK + ROPE_DIM]

    def scores(c):
        wait_all(b, c)
        slot = c % nslots
        kc = ckv_buf[slot].astype(BF)
        kpt = jnp.concatenate([kpe_buf[slot, p] for p in range(ppc)], axis=1).astype(BF)
        return kc, _dot_nt(q_lat, kc) + _dot(q_pe, kpt)

    cur = scores(0)
    for c in range(nch):
        nxt = c + ahead
        if nxt < nch:
            start_all(b, nxt)
        else:
            @pl.when(b + 1 < pl.num_programs(0))
            def _():
                start_all(b + 1, nxt - nch)
        following = scores(c + 1) if c + 1 < nch else None
        _softmax_step(cur[1], cur[0], m_ref, l_ref, acc_ref)
        cur = following

    kn = kvn_ref[0]
    rows = B_HEADS * t_new
    sn = _dot_nt(qs, kn)
    tq_ = lax.broadcasted_iota(jnp.int32, (rows, t_new), 0) % t_new
    tk_ = lax.broadcasted_iota(jnp.int32, (rows, t_new), 1)
    sn = jnp.where(tk_ <= tq_, sn, -jnp.inf)
    m_prev = m_ref[:, 0:1]
    m_new = jnp.maximum(m_prev, jnp.max(sn, axis=-1, keepdims=True))
    alpha = jnp.exp2((m_prev - m_new) * _EXP2_SCALE)
    pn = jnp.exp2((sn - m_new) * _EXP2_SCALE)
    l = alpha * jnp.sum(l_ref[...], axis=-1, keepdims=True) + jnp.sum(pn, axis=-1, keepdims=True)
    acc = alpha * acc_ref[...]
    vn = kn[:, :KV_RANK].astype(BF).astype(F32)
    pn = pn.astype(BF).astype(F32)
    for j in range(t_new):
        acc = acc + pn[:, j:j + 1] * vn[j:j + 1, :]
    o_lat = (acc / l).astype(BF)
    r = _dot(o_lat, wuv_ref[...])
    for h in range(B_HEADS):
        o_ref[0, :, h * V_DIM:(h + 1) * V_DIM] = r[h * t_new:(h + 1) * t_new, h * V_DIM:(h + 1) * V_DIM]


def _attn_sample(page_table, q_cat, kv_cat, cache_ckv, cache_kpe, wuv_all, nb, t_new, ppc=16, nslots=4):
    n_pages = page_table.shape[1]
    nch = n_pages // ppc
    assert nch % nslots == 0
    keys = ppc * PAGE_SIZE
    q3 = q_cat.reshape(nb, t_new, q_cat.shape[-1])
    kv3 = kv_cat.reshape(nb, t_new, kv_cat.shape[-1])
    rows = B_HEADS * t_new
    body = functools.partial(_attn_s_body, nch=nch, ppc=ppc, n_pages=n_pages, t_new=t_new, nslots=nslots)
    grid_spec = pltpu.PrefetchScalarGridSpec(
        num_scalar_prefetch=1,
        grid=(nb,),
        in_specs=[pl.BlockSpec((1, t_new, q3.shape[-1]), lambda bi, pt: (bi, 0, 0)),
                  pl.BlockSpec((1, t_new, QK_DIM), lambda bi, pt: (bi, 0, 0)),
                  pl.BlockSpec(memory_space=pl.ANY),
                  pl.BlockSpec(memory_space=pl.ANY),
                  pl.BlockSpec(wuv_all.shape, lambda bi, pt: (0, 0))],
        out_specs=pl.BlockSpec((1, t_new, B_HEADS * V_DIM), lambda bi, pt: (bi, 0, 0)),
        scratch_shapes=[pltpu.VMEM((nslots, keys, KV_RANK), F32),
                        pltpu.VMEM((nslots, ppc, ROPE_DIM, PAGE_SIZE), F32),
                        pltpu.SemaphoreType.DMA((nslots, 2)),
                        pltpu.VMEM((rows, QK_DIM), F32),
                        pltpu.VMEM((rows, KV_RANK), F32),
                        pltpu.VMEM((rows, LANES), F32),
                        pltpu.VMEM((rows, LANES), F32)])
    return pl.pallas_call(
        body,
        grid_spec=grid_spec,
        out_shape=jax.ShapeDtypeStruct((nb, t_new, B_HEADS * V_DIM), F32),
        compiler_params=_params(("arbitrary",)),
        name="attn_sample",
    )(page_table.reshape(-1), q3, kv3, cache_ckv, cache_kpe, wuv_all)


def _mix_body(oa_ref, ob_ref, ga_ref, gb_ref, x_ref, gt_ref, g_ref, wa_ref, wb_ref, wo_ref, o_ref):
    ya = _dot(oa_ref[...].astype(BF), wa_ref[...])
    yb = _dot(ob_ref[...].astype(BF), wb_ref[...])
    merged = jax.nn.sigmoid(ga_ref[...]) * ya + jax.nn.sigmoid(gb_ref[...]) * yb
    z = _dot(merged.astype(BF), wo_ref[...])
    x = x_ref[...]
    o_ref[...] = x + gt_ref[...] * _rms(z, g_ref[...]).reshape(x.shape)


def _mix(o_a, o_b, proj, x, mod, g, wa, wb, wo, gb, tr):
    bm, r, d = x.shape
    tm = gb * tr
    nr = r // tr
    gate0 = 4 * A_HEADS * A_DK // d

    def rows2(width, cb):
        return pl.BlockSpec((tm, width), lambda a, i: (a * nr + i, cb))

    def const(w):
        return pl.BlockSpec(w.shape, lambda a, i: (0, 0))

    return pl.pallas_call(
        _mix_body,
        grid=(bm // gb, nr),
        in_specs=[rows2(o_a.shape[1], 0), rows2(o_b.shape[1], 0),
                  rows2(d, gate0), rows2(d, gate0 + 1),
                  pl.BlockSpec((gb, tr, d), lambda a, i: (a, i, 0)),
                  pl.BlockSpec((gb, 1, d), lambda a, i: (a, 0, 2)),
                  const(g), const(wa), const(wb), const(wo)],
        out_specs=pl.BlockSpec((gb, tr, d), lambda a, i: (a, i, 0)),
        out_shape=jax.ShapeDtypeStruct(x.shape, F32),
        compiler_params=_params(("arbitrary", "arbitrary")),
        name="mix",
    )(o_a, o_b, proj, proj, x, mod, g, wa, wb, wo)


def _mlp_body(x_ref, sh_ref, sc_ref, gt_ref, g1_ref, g2_ref, wu_ref, wd_ref, o_ref, h_ref, acc_ref):
    f = pl.program_id(2)

    @pl.when(f == 0)
    def _():
        x = x_ref[...]
        h = _rms(x, g1_ref[...]) * (1.0 + sc_ref[...]) + sh_ref[...]
        h_ref[...] = h.reshape(h_ref.shape).astype(BF)
        acc_ref[...] = jnp.zeros_like(acc_ref)

    u = jnp.maximum(_dot(h_ref[...], wu_ref[...]), 0.0)
    acc_ref[...] += _dot((u * u).astype(BF), wd_ref[...])

    @pl.when(f == pl.num_programs(2) - 1)
    def _():
        x = x_ref[...]
        o_ref[...] = x + gt_ref[...] * _rms(acc_ref[...], g2_ref[...]).reshape(x.shape)


def _mlp(x, mod, g1, g2, wu, wd, gb, tr, tf=1024):
    bm, r, d = x.shape
    tm = gb * tr
    nr = r // tr
    ff = wu.shape[1]

    def modspec(piece):
        return pl.BlockSpec((gb, 1, d), lambda a, i, f: (a, 0, piece))

    return pl.pallas_call(
        _mlp_body,
        grid=(bm // gb, nr, ff // tf),
        in_specs=[pl.BlockSpec((gb, tr, d), lambda a, i, f: (a, i, 0)),
                  modspec(3), modspec(4), modspec(5),
                  pl.BlockSpec((1, d), lambda a, i, f: (0, 0)),
                  pl.BlockSpec((1, d), lambda a, i, f: (0, 0)),
                  pl.BlockSpec((d, tf), lambda a, i, f: (0, f)),
                  pl.BlockSpec((tf, d), lambda a, i, f: (f, 0))],
        out_specs=pl.BlockSpec((gb, tr, d), lambda a, i, f: (a, i, 0)),
        out_shape=jax.ShapeDtypeStruct(x.shape, F32),
        scratch_shapes=[pltpu.VMEM((tm, d), BF), pltpu.VMEM((tm, d), F32)],
        compiler_params=_params(("arbitrary", "arbitrary", "arbitrary")),
        name="mlp",
    )(x, mod, mod, mod, g1, g2, wu, wd)


def _rope_tables(pos):
    half = ROPE_DIM // 2
    inv = ROPE_THETA ** (-jnp.arange(half, dtype=F32) / half)
    ang = pos.astype(F32)[:, None] * inv[None, :]
    z = jnp.zeros((pos.shape[0], LANES - ROPE_DIM), F32)
    cos = jnp.cos(ang)
    sin = jnp.sin(ang)
    return jnp.concatenate([cos, cos, z], axis=1), jnp.concatenate([sin, sin, z], axis=1)


def _swap_halves(w):
    half = w.shape[-1] // 2
    return jnp.concatenate([-w[..., half:], w[..., :half]], axis=-1)


def _pad_lanes(w):
    pad = [(0, 0)] * (w.ndim - 1) + [(0, LANES - w.shape[-1])]
    return jnp.pad(w, pad)


def kernel(x_prompt, x_sample, c_prompt, c_sample, cache_ckv, cache_kpe, state_hgrn, page_table, w_ada, b_ada, g_pre_mix, g_post_mix, g_pre_mlp, g_post_mlp, w_in, lb_logits, g_hgrn_norm, w_a_out, g_q_norm, w_q_up, g_kv_norm, w_kv_up, w_b_out, w_o, w_up, w_down):
    depth = w_in.shape[0]
    assert depth == 1
    batch, seq, d = x_prompt.shape
    nb, t_new, _ = x_sample.shape
    past_len = page_table.shape[1] * PAGE_SIZE
    hk = A_HEADS * A_DK

    wit = w_in[0].T
    o_qd = 3 * hk + A_HEADS * A_DV
    o_kpe = o_qd + Q_RANK + KV_RANK
    o_gate = o_kpe + ROPE_DIM
    wit = wit.astype(BF)
    w_kpe = wit[o_kpe:o_gate]
    half = ROPE_DIM // 2
    w_kpe_swapped = jnp.concatenate([-w_kpe[half:], w_kpe[:half]], axis=0)
    zpad = jnp.zeros((LANES - ROPE_DIM, d), BF)
    w_mla = jnp.concatenate([wit[o_qd:o_kpe], w_kpe, zpad, w_kpe_swapped, zpad], axis=0)
    wq = w_q_up[0].reshape(Q_RANK, B_HEADS, NOPE_DIM + ROPE_DIM)
    wq_pe = wq[..., NOPE_DIM:]
    wq_cat = jnp.concatenate([wq[..., :NOPE_DIM].reshape(Q_RANK, -1),
                              _pad_lanes(wq_pe).reshape(Q_RANK, -1),
                              _pad_lanes(_swap_halves(wq_pe)).reshape(Q_RANK, -1)], axis=1).astype(BF)
    wkv = w_kv_up[0].reshape(KV_RANK, B_HEADS, NOPE_DIM + V_DIM)
    wuk = wkv[..., :NOPE_DIM].transpose(1, 2, 0).astype(BF)
    wuv = wkv[..., NOPE_DIM:].transpose(1, 0, 2).astype(BF)
    wuv_all = wkv[..., NOPE_DIM:].reshape(KV_RANK, B_HEADS * V_DIM).astype(BF)
    wa = w_a_out[0].astype(BF)
    wb = w_b_out[0].astype(BF)
    wo = w_o[0].astype(BF)
    wu = w_up[0].astype(BF)
    wd = w_down[0].astype(BF)

    c_all = jnp.concatenate([c_prompt, c_sample], axis=0)
    mod = _ada(c_all, w_ada[0], b_ada[0][None, :])
    mod = mod.reshape(batch + nb, 1, 6 * d)
    mod_p, mod_s = mod[:batch], mod[batch:]

    cos_p, sin_p = _rope_tables(jnp.arange(seq))
    cos_s, sin_s = _rope_tables(past_len + jnp.arange(t_new))
    gs = _TILE
    reps = gs
    cos_s = jnp.tile(cos_s, (reps, 1))
    sin_s = jnp.tile(sin_s, (reps, 1))

    def layer(x, mod_g, gb, tr, hgrn, attend, cos, sin, n_pos_tiles, act_dtype):
        tm = gb * tr
        proj = _proj(x, mod_g, g_pre_mix, wit, gb, tr, 2048, skip=(o_qd, o_gate - o_qd))
        proj_mla = _proj(x, mod_g, g_pre_mix, w_mla, gb, tr, w_mla.shape[0])
        o_a, s_fin = hgrn(proj)
        q_cat, kv_cat, ckv, kpe = _mla_prep(proj_mla, cos, sin, g_q_norm, g_kv_norm, wq_cat, wuk,
                                            tm, n_pos_tiles, act_dtype)
        o_b = attend(q_cat, kv_cat)
        mix_gb, mix_tr = (1, tr // 2) if gb == 1 else (gb // 2, tr)
        x1 = _mix(o_a.reshape(-1, o_a.shape[-1]), o_b.reshape(-1, o_b.shape[-1]), proj, x, mod_g,
                  g_post_mix, wa, wb, wo, mix_gb, mix_tr)
        y = _mlp(x1, mod_g, g_pre_mlp, g_post_mlp, wu, wd, gb, tr)
        return y, ckv, kpe, s_fin

    kpe_pages_t = jnp.swapaxes(cache_kpe[0], 1, 2)

    tr_p = 512
    y_p, ckv_p, kpe_p, s_p = layer(
        x_prompt, mod_p, 1, tr_p,
        lambda proj: _hgrn_prompt(proj, lb_logits, g_hgrn_norm, batch, seq),
        lambda q, kv: _attn_prompt(q, kv, wuv, batch, seq),
        cos_p, sin_p, seq // tr_p, BF)
    y_s, ckv_s, kpe_s, s_s = layer(
        x_sample, mod_s, gs, t_new,
        lambda proj: _hgrn_sample(proj, lb_logits, g_hgrn_norm, state_hgrn, nb, t_new),
        lambda q, kv: _attn_sample(page_table, q, kv, cache_ckv[0], kpe_pages_t, wuv_all, nb, t_new),
        cos_s, sin_s, 1, F32)

    return (y_p, y_s,
            ckv_p.reshape(1, batch, seq, KV_RANK), kpe_p.reshape(1, batch, seq, ROPE_DIM),
            s_p[None],
            ckv_s.reshape(1, nb, t_new, KV_RANK), kpe_s.reshape(1, nb, t_new, ROPE_DIM),
            s_s)
```
